```python
import functools
import jax, jax.numpy as jnp
from jax import lax
import numpy as np

D_MODEL = 1024
BATCH = 32
SEQ = 2048
DEPTH = 1
DEC_BATCH = 16
DEC_SEQ = 64
PAST_LEN = 2048

CHUNK = 64
WINDOW = 128
WIN_CHUNKS = WINDOW // CHUNK
ATTN_HEADS = 8
ATTN_KV_HEADS = 2
ATTN_GROUP = ATTN_HEADS // ATTN_KV_HEADS
ATTN_HEAD_DIM = 64
RET_HEADS = 4
RET_KEY_DIM = 128
RET_VALUE_DIM = 256
RET_ROPE_BASE = 10000.0
FFN_HIDDEN = ((8 * D_MODEL + 3 * 256 - 1) // (3 * 256)) * 256
PLE_DIM = 256
NORM_EPS = 1e-6
GN_EPS = 1e-5
IN_SIZES = (ATTN_HEADS * ATTN_HEAD_DIM, ATTN_KV_HEADS * ATTN_HEAD_DIM, ATTN_KV_HEADS * ATTN_HEAD_DIM,
            RET_HEADS * RET_KEY_DIM, RET_HEADS * RET_KEY_DIM, RET_HEADS * RET_VALUE_DIM,
            RET_HEADS * RET_VALUE_DIM, D_MODEL, D_MODEL)
D_IN = sum(IN_SIZES)

kernel_name = 'swa_sink_retention_hybrid_stream_step'


def rmsnorm(x, g):
    xf = x.astype(jnp.float32)
    y = xf * lax.rsqrt(jnp.mean(xf * xf, axis=-1, keepdims=True) + NORM_EPS)
    return (y * g.astype(jnp.float32)).astype(x.dtype)


def split_in(z):
    parts, start = [], 0
    for size in IN_SIZES:
        parts.append(z[..., start:start + size])
        start += size
    return parts


def rotary(x, pos):
    half = x.shape[-1] // 2
    inv = 1.0 / (RET_ROPE_BASE ** jnp.linspace(0.0, 1.0, half, dtype=jnp.float32))
    ang = pos.astype(jnp.float32)[:, None] * inv[None, :]
    cos = jnp.cos(ang)[None, :, None, :]
    sin = jnp.sin(ang)[None, :, None, :]
    x1, x2 = x[..., :half], x[..., half:]
    return jnp.concatenate([x1 * cos - x2 * sin, x1 * sin + x2 * cos], axis=-1)


def ret_log_decay():
    return jnp.log1p(-jnp.exp2(-5.0 - jnp.arange(RET_HEADS, dtype=jnp.float32)))


def sink_attention(q, k, v, valid, sinks):
    s = jnp.einsum('...qhgd,...khd->...hgqk', q, k,
                   preferred_element_type=jnp.float32) * (ATTN_HEAD_DIM ** -0.5)
    if valid is not None:
        s = jnp.where(valid[..., None, None, None, :], s, -jnp.inf)
    sink = sinks.astype(jnp.float32).reshape(ATTN_KV_HEADS, ATTN_GROUP, 1, 1)
    m = jnp.maximum(jnp.max(s, axis=-1, keepdims=True), sink)
    p = jnp.exp(s - m)
    w = p / (jnp.sum(p, axis=-1, keepdims=True) + jnp.exp(sink - m))
    return jnp.einsum('...hgqk,...khd->...qhgd', w.astype(v.dtype), v)


def attention_prompt(q, k, v, sinks):
    B, S = q.shape[0], q.shape[1]
    n_chunks = S // CHUNK
    qc = q.reshape(B, n_chunks, CHUNK, ATTN_KV_HEADS, ATTN_GROUP, ATTN_HEAD_DIM)
    pad = jnp.zeros((B, WIN_CHUNKS * CHUNK, ATTN_KV_HEADS, ATTN_HEAD_DIM), k.dtype)
    kc = jnp.concatenate([pad, k], axis=1).reshape(B, n_chunks + WIN_CHUNKS, CHUNK, ATTN_KV_HEADS, ATTN_HEAD_DIM)
    vc = jnp.concatenate([pad, v], axis=1).reshape(B, n_chunks + WIN_CHUNKS, CHUNK, ATTN_KV_HEADS, ATTN_HEAD_DIM)
    kb = jnp.concatenate([kc[:, j:j + n_chunks] for j in range(WIN_CHUNKS + 1)], axis=2)
    vb = jnp.concatenate([vc[:, j:j + n_chunks] for j in range(WIN_CHUNKS + 1)], axis=2)
    seg_chunk = jnp.arange(n_chunks)[:, None] + jnp.arange(WIN_CHUNKS + 1)[None, :] - WIN_CHUNKS
    valid = jnp.repeat(seg_chunk >= 0, CHUNK, axis=1)[None]
    o = sink_attention(qc, kb, vb, valid, sinks)
    o = o.reshape(B, S, ATTN_KV_HEADS, ATTN_GROUP, ATTN_HEAD_DIM)
    return o, k[:, -WINDOW:], v[:, -WINDOW:]


def attention_sample(q, k, v, sinks, cache_k, cache_v):
    n_win = cache_k.shape[1]
    kf = jnp.concatenate([cache_k.astype(k.dtype), k], axis=1)
    vf = jnp.concatenate([cache_v.astype(v.dtype), v], axis=1)
    o = sink_attention(q, kf, vf, None, sinks)
    return o, kf[:, -n_win:], vf[:, -n_win:]


def retention_block(state, q, k, v, log_g):
    L = q.shape[1]
    n = jnp.arange(L, dtype=jnp.float32)
    diff = n[:, None] - n[None, :]
    decay = jnp.where(diff >= 0, jnp.exp(log_g[:, None, None] * jnp.maximum(diff, 0.0)), 0.0)
    s = jnp.einsum('blhd,bmhd->bhlm', q, k) * decay
    intra = jnp.einsum('bhlm,bmhe->blhe', s, v)
    cross = jnp.einsum('blhd,bhde->blhe', q, state) * jnp.exp(log_g[None, :] * (n[:, None] + 1.0))[None, :, :, None]
    k_dec = jnp.exp(log_g[None, :] * (float(L) - 1.0 - n[:, None]))
    new_state = (jnp.exp(log_g * float(L))[None, :, None, None] * state
                 + jnp.einsum('blhd,blhe,lh->bhde', k, v, k_dec))
    return new_state, intra + cross


def retention_prompt(q, k, v):
    B, S, H = q.shape[0], q.shape[1], q.shape[2]
    n_chunks = S // CHUNK
    log_g = ret_log_decay()

    def to_chunks(t):
        return t.reshape(B, n_chunks, CHUNK, H, t.shape[-1]).swapaxes(0, 1)

    def step(st, xs):
        return retention_block(st, xs[0], xs[1], xs[2], log_g)

    state0 = jnp.zeros((B, H, RET_KEY_DIM, RET_VALUE_DIM), jnp.float32)
    state, o = lax.scan(step, state0, (to_chunks(q), to_chunks(k), to_chunks(v)))
    return o.swapaxes(0, 1).reshape(B, S, H, RET_VALUE_DIM), state


def retention_sample(q, k, v, state0):
    state, o = retention_block(state0.astype(jnp.float32), q, k, v, ret_log_decay())
    return o, state


def head_group_norm(o):
    mu = jnp.mean(o, axis=-1, keepdims=True)
    var = jnp.mean(jnp.square(o - mu), axis=-1, keepdims=True)
    return (o - mu) * lax.rsqrt(var + GN_EPS)


def layer(x, p, pos, attend, retain, g_mix_pre, w_in, attn_sinks, w_branch_attn, w_branch_ret,
          w_out, g_mix_post, g_ffn_pre, w_ffn_gate, w_ffn_up, w_ffn_down, g_ffn_post,
          w_ple_proj, w_ple_gate):
    B, L = x.shape[0], x.shape[1]
    h = rmsnorm(x, g_mix_pre)
    q_a, k_a, v_a, q_r, k_r, v_r, g_r, gate_a, gate_r = split_in(h @ w_in)
    o_a, k_state, v_state = attend(
        q_a.reshape(B, L, ATTN_KV_HEADS, ATTN_GROUP, ATTN_HEAD_DIM),
        k_a.reshape(B, L, ATTN_KV_HEADS, ATTN_HEAD_DIM),
        v_a.reshape(B, L, ATTN_KV_HEADS, ATTN_HEAD_DIM), attn_sinks)
    qr = rotary(q_r.reshape(B, L, RET_HEADS, RET_KEY_DIM).astype(jnp.float32), pos)
    kr = rotary(k_r.reshape(B, L, RET_HEADS, RET_KEY_DIM).astype(jnp.float32), pos) * (RET_KEY_DIM ** -0.5)
    vr = v_r.reshape(B, L, RET_HEADS, RET_VALUE_DIM).astype(jnp.float32)
    o_r, ret_state = retain(qr, kr, vr)
    o_r = head_group_norm(o_r).reshape(B, L, RET_HEADS * RET_VALUE_DIM).astype(x.dtype) * jax.nn.silu(g_r)
    merged = (jax.nn.sigmoid(gate_a) * (o_a.reshape(B, L, -1) @ w_branch_attn)
              + jax.nn.sigmoid(gate_r) * (o_r @ w_branch_ret))
    x = x + rmsnorm(merged @ w_out, g_mix_post)
    h = rmsnorm(x, g_ffn_pre)
    f = (jax.nn.silu(h @ w_ffn_gate) * (h @ w_ffn_up)) @ w_ffn_down
    x = x + rmsnorm(f, g_ffn_post)
    x = x + (p @ w_ple_proj) * jax.nn.sigmoid(x @ w_ple_gate)
    return x, k_state, v_state, ret_state


def setup_inputs(seed: int = 0) -> dict:
    key = jax.random.key(seed)
    ks = jax.random.split(key, 24)
    f32 = jnp.float32
    n_win = min(WINDOW, PAST_LEN)

    def nrm(k, shape, scale):
        return jax.random.normal(k, shape, f32) * scale

    def gain(k):
        return 1.0 + nrm(k, (DEPTH, D_MODEL), 0.01)

    return {
        'x_prompt': nrm(ks[0], (BATCH, SEQ, D_MODEL), 1.0),
        'x_sample': nrm(ks[1], (DEC_BATCH, DEC_SEQ, D_MODEL), 1.0),
        'p_prompt': nrm(ks[2], (DEPTH, BATCH, SEQ, PLE_DIM), 1.0),
        'p_sample': nrm(ks[3], (DEPTH, DEC_BATCH, DEC_SEQ, PLE_DIM), 1.0),
        'cache_attn_k': nrm(ks[4], (DEPTH, DEC_BATCH, n_win, ATTN_KV_HEADS, ATTN_HEAD_DIM), 1.0),
        'cache_attn_v': nrm(ks[5], (DEPTH, DEC_BATCH, n_win, ATTN_KV_HEADS, ATTN_HEAD_DIM), 1.0),
        'state_ret': nrm(ks[6], (DEPTH, DEC_BATCH, RET_HEADS, RET_KEY_DIM, RET_VALUE_DIM), 0.5),
        'g_mix_pre': gain(ks[7]),
        'w_in': nrm(ks[8], (DEPTH, D_MODEL, D_IN), D_MODEL ** -0.5),
        'attn_sinks': nrm(ks[9], (DEPTH, ATTN_HEADS), 1.0),
        'w_branch_attn': nrm(ks[10], (DEPTH, ATTN_HEADS * ATTN_HEAD_DIM, D_MODEL), (ATTN_HEADS * ATTN_HEAD_DIM) ** -0.5),
        'w_branch_ret': nrm(ks[11], (DEPTH, RET_HEADS * RET_VALUE_DIM, D_MODEL), (RET_HEADS * RET_VALUE_DIM) ** -0.5),
        'w_out': nrm(ks[12], (DEPTH, D_MODEL, D_MODEL), D_MODEL ** -0.5),
        'g_mix_post': gain(ks[13]),
        'g_ffn_pre': gain(ks[14]),
        'w_ffn_gate': nrm(ks[15], (DEPTH, D_MODEL, FFN_HIDDEN), D_MODEL ** -0.5),
        'w_ffn_up': nrm(ks[16], (DEPTH, D_MODEL, FFN_HIDDEN), D_MODEL ** -0.5),
        'w_ffn_down': nrm(ks[17], (DEPTH, FFN_HIDDEN, D_MODEL), FFN_HIDDEN ** -0.5),
        'g_ffn_post': gain(ks[18]),
        'w_ple_proj': nrm(ks[19], (DEPTH, PLE_DIM, D_MODEL), PLE_DIM ** -0.5),
        'w_ple_gate': nrm(ks[20], (DEPTH, D_MODEL, D_MODEL), D_MODEL ** -0.5),
    }


def reference(x_prompt, x_sample, p_prompt, p_sample, cache_attn_k, cache_attn_v, state_ret,
              g_mix_pre, w_in, attn_sinks, w_branch_attn, w_branch_ret, w_out, g_mix_post,
              g_ffn_pre, w_ffn_gate, w_ffn_up, w_ffn_down, g_ffn_post, w_ple_proj, w_ple_gate):
    pos_prompt = jnp.arange(x_prompt.shape[1])
    pos_sample = PAST_LEN + jnp.arange(x_sample.shape[1])
    y_p, y_s = x_prompt, x_sample
    kp_l, vp_l, rp_l, ks_l, vs_l, rs_l = [], [], [], [], [], []
    for i in range(DEPTH):
        w_i = (g_mix_pre[i], w_in[i], attn_sinks[i], w_branch_attn[i], w_branch_ret[i], w_out[i],
               g_mix_post[i], g_ffn_pre[i], w_ffn_gate[i], w_ffn_up[i], w_ffn_down[i], g_ffn_post[i],
               w_ple_proj[i], w_ple_gate[i])
        y_p, kp, vp, rp = layer(y_p, p_prompt[i], pos_prompt, attention_prompt, retention_prompt, *w_i)
        y_s, ksm, vsm, rsm = layer(
            y_s, p_sample[i], pos_sample,
            functools.partial(attention_sample, cache_k=cache_attn_k[i], cache_v=cache_attn_v[i]),
            functools.partial(retention_sample, state0=state_ret[i]), *w_i)
        kp_l.append(kp); vp_l.append(vp); rp_l.append(rp.astype(x_prompt.dtype))
        ks_l.append(ksm); vs_l.append(vsm); rs_l.append(rsm.astype(x_sample.dtype))
    return (y_p, y_s, jnp.stack(kp_l), jnp.stack(vp_l), jnp.stack(rp_l),
            jnp.stack(ks_l), jnp.stack(vs_l), jnp.stack(rs_l))
```

```python
import functools

import jax
import jax.numpy as jnp
import numpy as np
from jax import lax
from jax.experimental import pallas as pl
from jax.experimental.pallas import tpu as pltpu

CHUNK = 64
WINDOW = 128
ATTN_HEADS = 8
ATTN_KV_HEADS = 2
ATTN_GROUP = ATTN_HEADS // ATTN_KV_HEADS
ATTN_HEAD_DIM = 64
RET_HEADS = 4
RET_KEY_DIM = 128
RET_VALUE_DIM = 256
RET_ROPE_BASE = 10000.0
NORM_EPS = 1e-6
GN_EPS = 1e-5
PAST_LEN = 2048

Q_A = ATTN_HEADS * ATTN_HEAD_DIM
KV_A = ATTN_KV_HEADS * ATTN_HEAD_DIM
QK_R = RET_HEADS * RET_KEY_DIM
V_R = RET_HEADS * RET_VALUE_DIM

VMEM_LIMIT_BYTES = 56 * 1024 * 1024
TOKEN_TILE = 512
MIXER_TILE = 256
FFN_CHUNK = 512

BF16 = jnp.bfloat16
F32 = jnp.float32


def _dot(a, b):
    return jnp.dot(a, b, preferred_element_type=F32)


def _dot_nt(a, b):
    return lax.dot_general(a, b, (((1,), (1,)), ((), ())), preferred_element_type=F32)


def _dot_tn(a, b):
    return lax.dot_general(a, b, (((0,), (0,)), ((), ())), preferred_element_type=F32)


def _sigmoid(x):
    return 1.0 / (1.0 + jnp.exp(-x))


def _rmsnorm(x, g):
    return x * lax.rsqrt(jnp.mean(x * x, axis=-1, keepdims=True) + NORM_EPS) * g


def _resident(shape):
    return pl.BlockSpec(shape, lambda *_: (0,) * len(shape), pipeline_mode=pl.Buffered(1))


def _inproj_kernel(x_ref, g_ref, w_ref, cos_ref, sin_ref,
                   qa_ref, ka_ref, va_ref, qr_ref, kr_ref, vr_ref, gr_ref, gates_ref, kv32_ref,
                   *, tile, tail_rows, tiles_per_seq):
    h = _rmsnorm(x_ref[...], g_ref[...]).astype(BF16)

    qa_ref[...] = (_dot(h, w_ref[:, 0:Q_A]) * (ATTN_HEAD_DIM ** -0.5)).astype(BF16)
    kv = _dot(h, w_ref[:, Q_A:Q_A + 2 * KV_A])
    ka_ref[...] = kv[:, :KV_A].astype(BF16)
    va_ref[...] = kv[:, KV_A:].astype(BF16)
    if tiles_per_seq == 1:
        kv32_ref[...] = kv
    else:
        @pl.when(pl.program_id(0) % tiles_per_seq == tiles_per_seq - 1)
        def _():
            kv32_ref[...] = kv[tile - tail_rows:, :]

    cos = cos_ref[...]
    sin = sin_ref[...]
    base = Q_A + 2 * KV_A
    for dst, off, scale in ((qr_ref, base, None), (kr_ref, base + QK_R, RET_KEY_DIM ** -0.5)):
        z = _dot(h, w_ref[:, off:off + QK_R])
        for hh in range(RET_HEADS):
            zh = z[:, hh * RET_KEY_DIM:(hh + 1) * RET_KEY_DIM]
            rot = zh * cos + pltpu.roll(zh, RET_KEY_DIM // 2, 1) * sin
            if scale is not None:
                rot = rot * scale
            dst[:, hh * RET_KEY_DIM:(hh + 1) * RET_KEY_DIM] = rot.astype(BF16)

    off = base + 2 * QK_R
    vr_ref[...] = _dot(h, w_ref[:, off:off + V_R]).astype(BF16)
    off += V_R
    gr_ref[...] = _dot(h, w_ref[:, off:off + V_R]).astype(BF16)
    off += V_R
    d_model = x_ref.shape[1]
    for j in range(2):
        gates_ref[:, j * d_model:(j + 1) * d_model] = _dot(
            h, w_ref[:, off + j * d_model:off + (j + 1) * d_model]).astype(BF16)


def _in_projection(x2, g_pre, w_in_bf16, cos_tab, sin_tab, seq_len):
    n, d_model = x2.shape
    d_in = w_in_bf16.shape[1]
    tile = min(TOKEN_TILE, n)
    assert n % tile == 0
    if seq_len >= tile:
        assert seq_len % tile == 0 and tile >= WINDOW
        tiles_per_seq = seq_len // tile
        tail_rows = min(WINDOW, seq_len)
        kv32_rows = (n // seq_len) * tail_rows
        kv32_spec = pl.BlockSpec((tail_rows, 2 * KV_A), lambda i: (i // tiles_per_seq, 0))
        tab_blocks = tiles_per_seq
    else:
        assert tile % seq_len == 0 and seq_len <= WINDOW
        tiles_per_seq = 1
        tail_rows = tile
        kv32_rows = n
        kv32_spec = pl.BlockSpec((tile, 2 * KV_A), lambda i: (i, 0))
        cos_tab = jnp.tile(cos_tab, (tile // seq_len, 1))
        sin_tab = jnp.tile(sin_tab, (tile // seq_len, 1))
        tab_blocks = 1

    def row_spec(width):
        return pl.BlockSpec((tile, width), lambda i: (i, 0))

    tab_spec = pl.BlockSpec((tile, RET_KEY_DIM), lambda i: (i % tab_blocks, 0))
    out_widths = (Q_A, KV_A, KV_A, QK_R, QK_R, V_R, V_R, 2 * d_model)
    out_shape = [jax.ShapeDtypeStruct((n, w), BF16) for w in out_widths]
    out_shape.append(jax.ShapeDtypeStruct((kv32_rows, 2 * KV_A), F32))
    return pl.pallas_call(
        functools.partial(_inproj_kernel, tile=tile, tail_rows=tail_rows, tiles_per_seq=tiles_per_seq),
        grid=(n // tile,),
        in_specs=[row_spec(d_model), _resident((1, d_model)), _resident((d_model, d_in)), tab_spec, tab_spec],
        out_specs=[row_spec(w) for w in out_widths] + [kv32_spec],
        out_shape=out_shape,
        compiler_params=pltpu.CompilerParams(
            dimension_semantics=("arbitrary",), vmem_limit_bytes=VMEM_LIMIT_BYTES),
        name="in_projection",
    )(x2, g_pre, w_in_bf16, cos_tab, sin_tab)


def _mixer_kernel(*refs, tile, has_past, state_decay):
    (qa_ref, ka_ref, va_ref, qr_ref, kr_ref, vr_ref, gr_ref,
     sink_ref, dec_ref, cross_ref, kdec_ref) = refs[:11]
    rest = refs[11:]
    if has_past:
        ck_ref, cv_ref, st0_ref = rest[:3]
        rest = rest[3:]
    oa_ref, or_ref, st_out_ref, kbuf, vbuf, state = rest
    t = pl.program_id(1)
    n_chunks = tile // CHUNK

    @pl.when(t == 0)
    def _():
        if has_past:
            kbuf[0:WINDOW, :] = ck_ref[0]
            vbuf[0:WINDOW, :] = cv_ref[0]
            state[...] = st0_ref[0]
        else:
            kbuf[0:WINDOW, :] = jnp.zeros((WINDOW, KV_A), BF16)
            vbuf[0:WINDOW, :] = jnp.zeros((WINDOW, KV_A), BF16)
            state[...] = jnp.zeros(state.shape, F32)

    kbuf[WINDOW:, :] = ka_ref[...]
    vbuf[WINDOW:, :] = va_ref[...]

    lane = lax.broadcasted_iota(jnp.int32, (CHUNK, 2 * ATTN_HEAD_DIM), 1)
    low_half = lane < ATTN_HEAD_DIM
    key_col = lax.broadcasted_iota(jnp.int32, (1, WINDOW + CHUNK), 1)
    sink = sink_ref[...]

    for ci in range(n_chunks):
        rows = slice(ci * CHUNK, (ci + 1) * CHUNK)
        q_tiles = [qa_ref[rows, j * 128:(j + 1) * 128] for j in range(ATTN_GROUP)]
        zero = jnp.zeros_like(q_tiles[0])
        lhs = jnp.concatenate([jnp.where(low_half, q, zero) for q in q_tiles]
                              + [jnp.where(low_half, zero, q) for q in q_tiles], axis=0)
        k_win = kbuf[ci * CHUNK:ci * CHUNK + WINDOW + CHUNK, :]
        v_win = vbuf[ci * CHUNK:ci * CHUNK + WINDOW + CHUNK, :]
        s = _dot_nt(lhs, k_win)
        if not has_past:
            first_valid = (WINDOW // CHUNK - (t * n_chunks + ci)) * CHUNK
            s = s + jnp.where(key_col >= first_valid, 0.0, -jnp.inf)
        m = jnp.maximum(jnp.max(s, axis=1, keepdims=True), sink)
        p = jnp.exp(s - m)
        denom = jnp.sum(p, axis=1, keepdims=True) + jnp.exp(sink - m)
        o = _dot(p.astype(BF16), v_win) * (1.0 / denom)
        half = ATTN_GROUP * CHUNK
        for j in range(ATTN_GROUP):
            oa_ref[rows, j * 128:(j + 1) * 128] = jnp.where(
                low_half, o[j * CHUNK:(j + 1) * CHUNK], o[half + j * CHUNK:half + (j + 1) * CHUNK]).astype(BF16)

        for hh in range(RET_HEADS):
            kcols = slice(hh * RET_KEY_DIM, (hh + 1) * RET_KEY_DIM)
            vcols = slice(hh * RET_VALUE_DIM, (hh + 1) * RET_VALUE_DIM)
            q = qr_ref[rows, kcols]
            k = kr_ref[rows, kcols]
            v = vr_ref[rows, vcols]
            st = state[hh]
            intra = _dot((_dot_nt(q, k) * dec_ref[hh]).astype(BF16), v)
            cross = _dot(q, st.astype(BF16)) * cross_ref[hh]
            kd = (k.astype(F32) * kdec_ref[hh]).astype(BF16)
            state[hh] = state_decay[hh] * st + _dot_tn(kd, v)
            o_r = intra + cross
            mu = jnp.mean(o_r, axis=1, keepdims=True)
            cen = o_r - mu
            var = jnp.mean(cen * cen, axis=1, keepdims=True)
            g = gr_ref[rows, vcols].astype(F32)
            or_ref[rows, vcols] = (cen * lax.rsqrt(var + GN_EPS) * (g * _sigmoid(g))).astype(BF16)

    kbuf[0:WINDOW, :] = kbuf[tile:tile + WINDOW, :]
    vbuf[0:WINDOW, :] = vbuf[tile:tile + WINDOW, :]

    @pl.when(t == pl.num_programs(1) - 1)
    def _():
        st_out_ref[0] = state[...]


def _mixers(acts, sink_rows, tables, state_decay, batch, seq_len, past=None):
    qa, ka, va, qr, kr, vr, gr = acts
    tile = min(MIXER_TILE, seq_len)
    assert seq_len % tile == 0 and tile % CHUNK == 0
    nt = seq_len // tile
    n = batch * seq_len

    def row_spec(width):
        return pl.BlockSpec((tile, width), lambda b, t: (b * nt + t, 0))

    in_specs = [row_spec(a.shape[1]) for a in acts]
    in_specs.append(_resident(sink_rows.shape))
    in_specs += [_resident(tab.shape) for tab in tables]
    args = list(acts) + [sink_rows] + list(tables)
    if past is not None:
        ck, cv, st0 = past
        in_specs += [pl.BlockSpec((1, WINDOW, KV_A), lambda b, t: (b, 0, 0)),
                     pl.BlockSpec((1, WINDOW, KV_A), lambda b, t: (b, 0, 0)),
                     pl.BlockSpec((1, RET_HEADS, RET_KEY_DIM, RET_VALUE_DIM), lambda b, t: (b, 0, 0, 0))]
        args += [ck, cv, st0]
    state_shape = (RET_HEADS, RET_KEY_DIM, RET_VALUE_DIM)
    return pl.pallas_call(
        functools.partial(_mixer_kernel, tile=tile, has_past=past is not None, state_decay=state_decay),
        grid=(batch, nt),
        in_specs=in_specs,
        out_specs=[row_spec(Q_A), row_spec(V_R),
                   pl.BlockSpec((1,) + state_shape, lambda b, t: (b, 0, 0, 0))],
        out_shape=[jax.ShapeDtypeStruct((n, Q_A), BF16), jax.ShapeDtypeStruct((n, V_R), BF16),
                   jax.ShapeDtypeStruct((batch,) + state_shape, F32)],
        scratch_shapes=[pltpu.VMEM((WINDOW + tile, KV_A), BF16), pltpu.VMEM((WINDOW + tile, KV_A), BF16),
                        pltpu.VMEM(state_shape, F32)],
        compiler_params=pltpu.CompilerParams(
            dimension_semantics=("arbitrary", "arbitrary"), vmem_limit_bytes=VMEM_LIMIT_BYTES),
        name="mixers",
    )(*args)


def _post_kernel(oa_ref, or_ref, gates_ref, x_ref, p_ref,
                 wba_ref, wbr_ref, wout_ref, gpost_ref, gfpre_ref, wg_ref, wu_ref, wd_ref, gfpost_ref,
                 wpp_ref, wpg_ref, y_ref, *, ffn_slabs):
    d_model = x_ref.shape[1]
    gate_a = gates_ref[:, :d_model].astype(F32)
    gate_r = gates_ref[:, d_model:].astype(F32)
    merged = (_sigmoid(gate_a) * _dot(oa_ref[...], wba_ref[...])
              + _sigmoid(gate_r) * _dot(or_ref[...], wbr_ref[...]))
    y = x_ref[...] + _rmsnorm(_dot(merged.astype(BF16), wout_ref[...]), gpost_ref[...])

    h = _rmsnorm(y, gfpre_ref[...]).astype(BF16)
    f = None
    for lo, hi in ffn_slabs:
        gate = _dot(h, wg_ref[:, lo:hi])
        act = (gate * _sigmoid(gate) * _dot(h, wu_ref[:, lo:hi])).astype(BF16)
        part = _dot(act, wd_ref[lo:hi, :])
        f = part if f is None else f + part
    y = y + _rmsnorm(f, gfpost_ref[...])

    emb = _dot(p_ref[...].astype(BF16), wpp_ref[...])
    y_ref[...] = y + emb * _sigmoid(_dot(y.astype(BF16), wpg_ref[...]))


def _output_stage(oa, orr, gates, x2, p2, weights):
    n, d_model = x2.shape
    tile = min(TOKEN_TILE, n)
    assert n % tile == 0
    ffn_hidden = weights[7].shape[0]
    ffn_slabs = tuple((lo, min(lo + FFN_CHUNK, ffn_hidden)) for lo in range(0, ffn_hidden, FFN_CHUNK))

    def row_spec(width):
        return pl.BlockSpec((tile, width), lambda i: (i, 0))

    acts = (oa, orr, gates, x2, p2)
    return pl.pallas_call(
        functools.partial(_post_kernel, ffn_slabs=ffn_slabs),
        grid=(n // tile,),
        in_specs=[row_spec(a.shape[1]) for a in acts] + [_resident(w.shape) for w in weights],
        out_specs=row_spec(d_model),
        out_shape=jax.ShapeDtypeStruct((n, d_model), F32),
        compiler_params=pltpu.CompilerParams(
            dimension_semantics=("arbitrary",), vmem_limit_bytes=VMEM_LIMIT_BYTES),
        name="output_stage",
    )(*acts, *weights)


def _rotary_tables(pos):
    half = RET_KEY_DIM // 2
    inv = 1.0 / (RET_ROPE_BASE ** jnp.linspace(0.0, 1.0, half, dtype=F32))
    ang = pos.astype(F32)[:, None] * inv[None, :]
    cos, sin = jnp.cos(ang), jnp.sin(ang)
    return jnp.concatenate([cos, cos], axis=1), jnp.concatenate([-sin, sin], axis=1)


def _retention_tables():
    log_g = jnp.log1p(-jnp.exp2(-5.0 - jnp.arange(RET_HEADS, dtype=F32)))
    idx = jnp.arange(CHUNK, dtype=F32)
    diff = idx[:, None] - idx[None, :]
    decay = jnp.where(diff >= 0, jnp.exp(log_g[:, None, None] * jnp.maximum(diff, 0.0)), 0.0)
    cross = jnp.exp(log_g[:, None] * (idx[None, :] + 1.0))
    k_dec = jnp.exp(log_g[:, None] * (float(CHUNK) - 1.0 - idx[None, :]))
    cross = jnp.broadcast_to(cross[:, :, None], (RET_HEADS, CHUNK, RET_VALUE_DIM))
    k_dec = jnp.broadcast_to(k_dec[:, :, None], (RET_HEADS, CHUNK, RET_KEY_DIM))
    return decay, cross, k_dec


def _state_decay():
    log_g = np.log1p(-np.exp2(-5.0 - np.arange(RET_HEADS, dtype=np.float64)))
    return tuple(float(v) for v in np.exp(log_g * CHUNK))


def _head_pair_order():
    order = []
    for j in range(ATTN_GROUP):
        for kvh in range(ATTN_KV_HEADS):
            head = kvh * ATTN_GROUP + j
            order.extend(range(head * ATTN_HEAD_DIM, (head + 1) * ATTN_HEAD_DIM))
    return np.asarray(order, dtype=np.int32)


def _layer(x, p, pos, past, w):
    (g_mix_pre, w_in, attn_sinks, w_branch_attn, w_branch_ret, w_out, g_mix_post, g_ffn_pre,
     w_ffn_gate, w_ffn_up, w_ffn_down, g_ffn_post, w_ple_proj, w_ple_gate) = w
    batch, seq_len, d_model = x.shape
    n = batch * seq_len
    x2 = x.reshape(n, d_model)
    p2 = p.reshape(n, p.shape[-1])

    order = _head_pair_order()
    w_in_b = jnp.concatenate([w_in[:, :Q_A][:, order], w_in[:, Q_A:]], axis=1).astype(BF16)
    cos_tab, sin_tab = _rotary_tables(pos)
    outs = _in_projection(x2, g_mix_pre.reshape(1, d_model), w_in_b, cos_tab, sin_tab, seq_len)
    qa, ka, va, qr, kr, vr, gr, gates, kv32 = outs

    sink_rows = jnp.repeat(attn_sinks.astype(F32), CHUNK).reshape(ATTN_HEADS * CHUNK, 1)
    past_args = None
    if past is not None:
        cache_k, cache_v, state0 = past
        past_args = (cache_k.reshape(batch, WINDOW, KV_A).astype(BF16),
                     cache_v.reshape(batch, WINDOW, KV_A).astype(BF16), state0.astype(F32))
    oa, orr, state = _mixers((qa, ka, va, qr, kr, vr, gr), sink_rows, _retention_tables(), _state_decay(),
                             batch, seq_len, past_args)

    weights = (w_branch_attn[order, :].astype(BF16), w_branch_ret.astype(BF16), w_out.astype(BF16),
               g_mix_post.reshape(1, d_model), g_ffn_pre.reshape(1, d_model),
               w_ffn_gate.astype(BF16), w_ffn_up.astype(BF16), w_ffn_down.astype(BF16),
               g_ffn_post.reshape(1, d_model), w_ple_proj.astype(BF16), w_ple_gate.astype(BF16))
    y = _output_stage(oa, orr, gates, x2, p2, weights).reshape(batch, seq_len, d_model)

    tail = kv32.shape[0] // batch
    kv32 = kv32.reshape(batch, tail, 2, ATTN_KV_HEADS, ATTN_HEAD_DIM)
    k_new, v_new = kv32[:, :, 0], kv32[:, :, 1]
    if past is not None:
        n_win = past[0].shape[1]
        k_new = jnp.concatenate([past[0], k_new], axis=1)[:, -n_win:]
        v_new = jnp.concatenate([past[1], v_new], axis=1)[:, -n_win:]
    return y, k_new, v_new, state


def kernel(x_prompt, x_sample, p_prompt, p_sample, cache_attn_k, cache_attn_v, state_ret, g_mix_pre, w_in, attn_sinks, w_branch_attn, w_branch_ret, w_out, g_mix_post, g_ffn_pre, w_ffn_gate, w_ffn_up, w_ffn_down, g_ffn_post, w_ple_proj, w_ple_gate):
    depth = w_in.shape[0]
    assert cache_attn_k.shape[2] == WINDOW, "the rolling window must be full"
    pos_prompt = jnp.arange(x_prompt.shape[1])
    pos_sample = PAST_LEN + jnp.arange(x_sample.shape[1])
    y_p, y_s = x_prompt, x_sample
    outs = [[] for _ in range(6)]
    for i in range(depth):
        w_i = (g_mix_pre[i], w_in[i], attn_sinks[i], w_branch_attn[i], w_branch_ret[i], w_out[i],
               g_mix_post[i], g_ffn_pre[i], w_ffn_gate[i], w_ffn_up[i], w_ffn_down[i], g_ffn_post[i],
               w_ple_proj[i], w_ple_gate[i])
        y_p, kp, vp, rp = _layer(y_p, p_prompt[i], pos_prompt, None, w_i)
        y_s, ks, vs, rs = _layer(y_s, p_sample[i], pos_sample,
                                 (cache_attn_k[i], cache_attn_v[i], state_ret[i]), w_i)
        for lst, val in zip(outs, (kp, vp, rp.astype(x_prompt.dtype), ks, vs, rs.astype(x_sample.dtype))):
            lst.append(val)
    return (y_p, y_s) + tuple(jnp.stack(lst) for lst in outs)
```

```python
import functools

import jax
import jax.numpy as jnp
import numpy as np
from jax import lax
from jax.experimental import pallas as pl
from jax.experimental.pallas import tpu as pltpu

CHUNK = 64
WINDOW = 128
ATTN_HEADS = 8
ATTN_KV_HEADS = 2
ATTN_GROUP = ATTN_HEADS // ATTN_KV_HEADS
ATTN_HEAD_DIM = 64
RET_HEADS = 4
RET_KEY_DIM = 128
RET_VALUE_DIM = 256
RET_ROPE_BASE = 10000.0
NORM_EPS = 1e-6
GN_EPS = 1e-5
PAST_LEN = 2048

Q_A = ATTN_HEADS * ATTN_HEAD_DIM
KV_A = ATTN_KV_HEADS * ATTN_HEAD_DIM
QK_R = RET_HEADS * RET_KEY_DIM
V_R = RET_HEADS * RET_VALUE_DIM

VMEM_LIMIT_BYTES = 56 * 1024 * 1024
TOKEN_TILE = 512
MIXER_TILE = 256
RET_BLOCK = 256
ATT_GROUP = 128
FFN_CHUNK = 512

BF16 = jnp.bfloat16
F32 = jnp.float32


def _dot(a, b):
    return jnp.dot(a, b, preferred_element_type=F32)


def _dot_nt(a, b):
    return lax.dot_general(a, b, (((1,), (1,)), ((), ())), preferred_element_type=F32)


def _dot_tn(a, b):
    return lax.dot_general(a, b, (((0,), (0,)), ((), ())), preferred_element_type=F32)


def _sigmoid(x):
    return 1.0 / (1.0 + jnp.exp(-x))


def _rmsnorm(x, g):
    return x * lax.rsqrt(jnp.mean(x * x, axis=-1, keepdims=True) + NORM_EPS) * g


def _resident(shape):
    return pl.BlockSpec(shape, lambda *_: (0,) * len(shape), pipeline_mode=pl.Buffered(1))


def _inproj_kernel(x_ref, g_ref, w_ref, cos_ref, sin_ref,
                   qa_ref, ka_ref, va_ref, qr_ref, kr_ref, vr_ref, gr_ref, gates_ref, kv32_ref,
                   *, tile, tail_rows, tiles_per_seq):
    h = _rmsnorm(x_ref[...], g_ref[...]).astype(BF16)

    qa_ref[...] = (_dot(h, w_ref[:, 0:Q_A]) * (ATTN_HEAD_DIM ** -0.5)).astype(BF16)
    kv = _dot(h, w_ref[:, Q_A:Q_A + 2 * KV_A])
    ka_ref[...] = kv[:, :KV_A].astype(BF16)
    va_ref[...] = kv[:, KV_A:].astype(BF16)
    if tiles_per_seq == 1:
        kv32_ref[...] = kv
    else:
        @pl.when(pl.program_id(0) % tiles_per_seq == tiles_per_seq - 1)
        def _():
            kv32_ref[...] = kv[tile - tail_rows:, :]

    cos = cos_ref[...]
    sin = sin_ref[...]
    base = Q_A + 2 * KV_A
    for dst, off, scale in ((qr_ref, base, None), (kr_ref, base + QK_R, RET_KEY_DIM ** -0.5)):
        z = _dot(h, w_ref[:, off:off + QK_R])
        for hh in range(RET_HEADS):
            zh = z[:, hh * RET_KEY_DIM:(hh + 1) * RET_KEY_DIM]
            rot = zh * cos + pltpu.roll(zh, RET_KEY_DIM // 2, 1) * sin
            if scale is not None:
                rot = rot * scale
            dst[:, hh * RET_KEY_DIM:(hh + 1) * RET_KEY_DIM] = rot.astype(BF16)

    off = base + 2 * QK_R
    vr_ref[...] = _dot(h, w_ref[:, off:off + V_R]).astype(BF16)
    off += V_R
    gr_ref[...] = _dot(h, w_ref[:, off:off + V_R]).astype(BF16)
    off += V_R
    d_model = x_ref.shape[1]
    for j in range(2):
        gates_ref[:, j * d_model:(j + 1) * d_model] = _dot(
            h, w_ref[:, off + j * d_model:off + (j + 1) * d_model]).astype(BF16)


def _in_projection(x2, g_pre, w_in_bf16, cos_tab, sin_tab, seq_len):
    n, d_model = x2.shape
    d_in = w_in_bf16.shape[1]
    tile = min(TOKEN_TILE, n)
    assert n % tile == 0
    if seq_len >= tile:
        assert seq_len % tile == 0 and tile >= WINDOW
        tiles_per_seq = seq_len // tile
        tail_rows = min(WINDOW, seq_len)
        kv32_rows = (n // seq_len) * tail_rows
        kv32_spec = pl.BlockSpec((tail_rows, 2 * KV_A), lambda i: (i // tiles_per_seq, 0))
        tab_blocks = tiles_per_seq
    else:
        assert tile % seq_len == 0 and seq_len <= WINDOW
        tiles_per_seq = 1
        tail_rows = tile
        kv32_rows = n
        kv32_spec = pl.BlockSpec((tile, 2 * KV_A), lambda i: (i, 0))
        cos_tab = jnp.tile(cos_tab, (tile // seq_len, 1))
        sin_tab = jnp.tile(sin_tab, (tile // seq_len, 1))
        tab_blocks = 1

    def row_spec(width):
        return pl.BlockSpec((tile, width), lambda i: (i, 0))

    tab_spec = pl.BlockSpec((tile, RET_KEY_DIM), lambda i: (i % tab_blocks, 0))
    out_widths = (Q_A, KV_A, KV_A, QK_R, QK_R, V_R, V_R, 2 * d_model)
    out_shape = [jax.ShapeDtypeStruct((n, w), BF16) for w in out_widths]
    out_shape.append(jax.ShapeDtypeStruct((kv32_rows, 2 * KV_A), F32))
    return pl.pallas_call(
        functools.partial(_inproj_kernel, tile=tile, tail_rows=tail_rows, tiles_per_seq=tiles_per_seq),
        grid=(n // tile,),
        in_specs=[row_spec(d_model), _resident((1, d_model)), _resident((d_model, d_in)), tab_spec, tab_spec],
        out_specs=[row_spec(w) for w in out_widths] + [kv32_spec],
        out_shape=out_shape,
        compiler_params=pltpu.CompilerParams(
            dimension_semantics=("arbitrary",), vmem_limit_bytes=VMEM_LIMIT_BYTES),
        name="in_projection",
    )(x2, g_pre, w_in_bf16, cos_tab, sin_tab)


def _mixer_kernel(*refs, tile, ret_block, att_group, has_past, use_bias, state_decay):
    (qa_ref, ka_ref, va_ref, qr_ref, kr_ref, vr_ref, gr_ref,
     sink_ref, bias_ref, dec_ref, cross_ref, kdec_ref) = refs[:12]
    rest = refs[12:]
    if has_past:
        ck_ref, cv_ref, st0_ref = rest[:3]
        rest = rest[3:]
    oa_ref, or_ref, st_out_ref, kbuf, vbuf, state = rest
    t = pl.program_id(1)

    @pl.when(t == 0)
    def _():
        if has_past:
            kbuf[0:WINDOW, :] = ck_ref[0]
            vbuf[0:WINDOW, :] = cv_ref[0]
            state[...] = st0_ref[0]
        else:
            kbuf[0:WINDOW, :] = jnp.zeros((WINDOW, KV_A), BF16)
            vbuf[0:WINDOW, :] = jnp.zeros((WINDOW, KV_A), BF16)
            state[...] = jnp.zeros(state.shape, F32)

    kbuf[WINDOW:, :] = ka_ref[...]
    vbuf[WINDOW:, :] = va_ref[...]

    low_half = lax.broadcasted_iota(jnp.int32, (att_group, 2 * ATTN_HEAD_DIM), 1) < ATTN_HEAD_DIM
    win = WINDOW + att_group
    for gi in range(tile // att_group):
        rows = slice(gi * att_group, (gi + 1) * att_group)
        q_tiles = [qa_ref[rows, j * 128:(j + 1) * 128] for j in range(ATTN_GROUP)]
        zero = jnp.zeros_like(q_tiles[0])
        lhs = jnp.concatenate([jnp.where(low_half, q, zero) for q in q_tiles]
                              + [jnp.where(low_half, zero, q) for q in q_tiles], axis=0)
        k_win = kbuf[gi * att_group:gi * att_group + win, :]
        v_win = vbuf[gi * att_group:gi * att_group + win, :]
        s_all = _dot_nt(lhs, k_win)
        if use_bias:
            bias = bias_ref[0] if (has_past or gi > 0) else bias_ref[jnp.where(t == 0, 1, 0)]
        probs, inv_denoms = [], []
        for r in range(ATTN_HEADS):
            s = s_all[r * att_group:(r + 1) * att_group]
            if use_bias:
                s = s + bias
            sink = sink_ref[r]
            m = jnp.maximum(jnp.max(s, axis=1, keepdims=True), sink)
            p = jnp.exp(s - m)
            inv_denoms.append(1.0 / (jnp.sum(p, axis=1, keepdims=True) + jnp.exp(sink - m)))
            probs.append(p.astype(BF16))
        o = _dot(jnp.concatenate(probs, axis=0), v_win)
        for j in range(ATTN_GROUP):
            r_lo, r_hi = j, ATTN_GROUP + j
            lo = o[r_lo * att_group:(r_lo + 1) * att_group] * inv_denoms[r_lo]
            hi = o[r_hi * att_group:(r_hi + 1) * att_group] * inv_denoms[r_hi]
            oa_ref[rows, j * 128:(j + 1) * 128] = jnp.where(low_half, lo, hi).astype(BF16)

    for bi in range(tile // ret_block):
        rows = slice(bi * ret_block, (bi + 1) * ret_block)
        for hh in range(RET_HEADS):
            kcols = slice(hh * RET_KEY_DIM, (hh + 1) * RET_KEY_DIM)
            vcols = slice(hh * RET_VALUE_DIM, (hh + 1) * RET_VALUE_DIM)
            q = qr_ref[rows, kcols]
            k = kr_ref[rows, kcols]
            v = vr_ref[rows, vcols]
            st = state[hh]
            intra = _dot((_dot_nt(q, k) * dec_ref[hh]).astype(BF16), v)
            cross = _dot(q, st.astype(BF16)) * cross_ref[hh]
            kd = (k.astype(F32) * kdec_ref[hh]).astype(BF16)
            state[hh] = state_decay[hh] * st + _dot_tn(kd, v)
            o_r = intra + cross
            mu = jnp.mean(o_r, axis=1, keepdims=True)
            cen = o_r - mu
            var = jnp.mean(cen * cen, axis=1, keepdims=True)
            g = gr_ref[rows, vcols].astype(F32)
            or_ref[rows, vcols] = (cen * lax.rsqrt(var + GN_EPS) * (g * _sigmoid(g))).astype(BF16)

    kbuf[0:WINDOW, :] = kbuf[tile:tile + WINDOW, :]
    vbuf[0:WINDOW, :] = vbuf[tile:tile + WINDOW, :]

    @pl.when(t == pl.num_programs(1) - 1)
    def _():
        st_out_ref[0] = state[...]


def _mixers(acts, sinks, batch, seq_len, past=None):
    tile = min(MIXER_TILE, seq_len)
    ret_block = min(RET_BLOCK, tile)
    att_group = min(ATT_GROUP, tile)
    assert seq_len % tile == 0 and tile % ret_block == 0 and tile % att_group == 0 and att_group % CHUNK == 0
    nt = seq_len // tile
    n = batch * seq_len
    use_bias = not (past is not None and att_group == CHUNK)
    tables = (_attention_bias(att_group),) + _retention_tables(ret_block)

    def row_spec(width):
        return pl.BlockSpec((tile, width), lambda b, t: (b * nt + t, 0))

    in_specs = [row_spec(a.shape[1]) for a in acts]
    in_specs.append(pl.BlockSpec(memory_space=pltpu.SMEM))
    in_specs += [_resident(tab.shape) for tab in tables]
    args = list(acts) + [sinks] + list(tables)
    if past is not None:
        in_specs += [pl.BlockSpec((1, WINDOW, KV_A), lambda b, t: (b, 0, 0)),
                     pl.BlockSpec((1, WINDOW, KV_A), lambda b, t: (b, 0, 0)),
                     pl.BlockSpec((1, RET_HEADS, RET_KEY_DIM, RET_VALUE_DIM), lambda b, t: (b, 0, 0, 0))]
        args += list(past)
    state_shape = (RET_HEADS, RET_KEY_DIM, RET_VALUE_DIM)
    return pl.pallas_call(
        functools.partial(_mixer_kernel, tile=tile, ret_block=ret_block, att_group=att_group,
                          has_past=past is not None, use_bias=use_bias, state_decay=_state_decay(ret_block)),
        grid=(batch, nt),
        in_specs=in_specs,
        out_specs=[row_spec(Q_A), row_spec(V_R),
                   pl.BlockSpec((1,) + state_shape, lambda b, t: (b, 0, 0, 0))],
        out_shape=[jax.ShapeDtypeStruct((n, Q_A), BF16), jax.ShapeDtypeStruct((n, V_R), BF16),
                   jax.ShapeDtypeStruct((batch,) + state_shape, F32)],
        scratch_shapes=[pltpu.VMEM((WINDOW + tile, KV_A), BF16), pltpu.VMEM((WINDOW + tile, KV_A), BF16),
                        pltpu.VMEM(state_shape, F32)],
        compiler_params=pltpu.CompilerParams(
            dimension_semantics=("arbitrary", "arbitrary"), vmem_limit_bytes=VMEM_LIMIT_BYTES),
        name="mixers",
    )(*args)


def _post_kernel(oa_ref, or_ref, gates_ref, x_ref, p_ref,
                 wba_ref, wbr_ref, wout_ref, gpost_ref, gfpre_ref, wg_ref, wu_ref, wd_ref, gfpost_ref,
                 wpp_ref, wpg_ref, y_ref, *, ffn_slabs):
    d_model = x_ref.shape[1]
    gate_a = gates_ref[:, :d_model].astype(F32)
    gate_r = gates_ref[:, d_model:].astype(F32)
    merged = (_sigmoid(gate_a) * _dot(oa_ref[...], wba_ref[...])
              + _sigmoid(gate_r) * _dot(or_ref[...], wbr_ref[...]))
    y = x_ref[...] + _rmsnorm(_dot(merged.astype(BF16), wout_ref[...]), gpost_ref[...])

    h = _rmsnorm(y, gfpre_ref[...]).astype(BF16)
    f = None
    for lo, hi in ffn_slabs:
        gate = _dot(h, wg_ref[:, lo:hi])
        act = (gate * _sigmoid(gate) * _dot(h, wu_ref[:, lo:hi])).astype(BF16)
        part = _dot(act, wd_ref[lo:hi, :])
        f = part if f is None else f + part
    y = y + _rmsnorm(f, gfpost_ref[...])

    emb = _dot(p_ref[...].astype(BF16), wpp_ref[...])
    y_ref[...] = y + emb * _sigmoid(_dot(y.astype(BF16), wpg_ref[...]))


def _output_stage(oa, orr, gates, x2, p2, weights):
    n, d_model = x2.shape
    tile = min(TOKEN_TILE, n)
    assert n % tile == 0
    ffn_hidden = weights[7].shape[0]
    ffn_slabs = tuple((lo, min(lo + FFN_CHUNK, ffn_hidden)) for lo in range(0, ffn_hidden, FFN_CHUNK))

    def row_spec(width):
        return pl.BlockSpec((tile, width), lambda i: (i, 0))

    acts = (oa, orr, gates, x2, p2)
    return pl.pallas_call(
        functools.partial(_post_kernel, ffn_slabs=ffn_slabs),
        grid=(n // tile,),
        in_specs=[row_spec(a.shape[1]) for a in acts] + [_resident(w.shape) for w in weights],
        out_specs=row_spec(d_model),
        out_shape=jax.ShapeDtypeStruct((n, d_model), F32),
        compiler_params=pltpu.CompilerParams(
            dimension_semantics=("arbitrary",), vmem_limit_bytes=VMEM_LIMIT_BYTES),
        name="output_stage",
    )(*acts, *weights)


def _rotary_tables(pos):
    half = RET_KEY_DIM // 2
    inv = 1.0 / (RET_ROPE_BASE ** jnp.linspace(0.0, 1.0, half, dtype=F32))
    ang = pos.astype(F32)[:, None] * inv[None, :]
    cos, sin = jnp.cos(ang), jnp.sin(ang)
    return jnp.concatenate([cos, cos], axis=1), jnp.concatenate([-sin, sin], axis=1)


def _retention_tables(block):
    log_g = jnp.log1p(-jnp.exp2(-5.0 - jnp.arange(RET_HEADS, dtype=F32)))
    idx = jnp.arange(block, dtype=F32)
    diff = idx[:, None] - idx[None, :]
    decay = jnp.where(diff >= 0, jnp.exp(log_g[:, None, None] * jnp.maximum(diff, 0.0)), 0.0)
    cross = jnp.exp(log_g[:, None] * (idx[None, :] + 1.0))
    k_dec = jnp.exp(log_g[:, None] * (float(block) - 1.0 - idx[None, :]))
    cross = jnp.broadcast_to(cross[:, :, None], (RET_HEADS, block, RET_VALUE_DIM))
    k_dec = jnp.broadcast_to(k_dec[:, :, None], (RET_HEADS, block, RET_KEY_DIM))
    return decay, cross, k_dec


def _state_decay(block):
    log_g = np.log1p(-np.exp2(-5.0 - np.arange(RET_HEADS, dtype=np.float64)))
    return tuple(float(v) for v in np.exp(log_g * block))


def _attention_bias(group):
    q_chunk = np.arange(group)[:, None] // CHUNK
    k_chunk = np.arange(WINDOW + group)[None, :] // CHUNK
    visible = (k_chunk >= q_chunk) & (k_chunk <= q_chunk + WINDOW // CHUNK)
    at_start = visible & (k_chunk >= WINDOW // CHUNK)
    return jnp.asarray(np.where(np.stack([visible, at_start]), 0.0, -np.inf), dtype=F32)


def _head_pair_order():
    order = []
    for j in range(ATTN_GROUP):
        for kvh in range(ATTN_KV_HEADS):
            head = kvh * ATTN_GROUP + j
            order.extend(range(head * ATTN_HEAD_DIM, (head + 1) * ATTN_HEAD_DIM))
    return np.asarray(order, dtype=np.int32)


def _layer(x, p, pos, past, w):
    (g_mix_pre, w_in, attn_sinks, w_branch_attn, w_branch_ret, w_out, g_mix_post, g_ffn_pre,
     w_ffn_gate, w_ffn_up, w_ffn_down, g_ffn_post, w_ple_proj, w_ple_gate) = w
    batch, seq_len, d_model = x.shape
    n = batch * seq_len
    x2 = x.reshape(n, d_model)
    p2 = p.reshape(n, p.shape[-1])

    order = _head_pair_order()
    w_in_b = jnp.concatenate([w_in[:, :Q_A][:, order], w_in[:, Q_A:]], axis=1).astype(BF16)
    cos_tab, sin_tab = _rotary_tables(pos)
    outs = _in_projection(x2, g_mix_pre.reshape(1, d_model), w_in_b, cos_tab, sin_tab, seq_len)
    qa, ka, va, qr, kr, vr, gr, gates, kv32 = outs

    past_args = None
    if past is not None:
        cache_k, cache_v, state0 = past
        past_args = (cache_k.reshape(batch, WINDOW, KV_A).astype(BF16),
                     cache_v.reshape(batch, WINDOW, KV_A).astype(BF16), state0.astype(F32))
    oa, orr, state = _mixers((qa, ka, va, qr, kr, vr, gr), attn_sinks.astype(F32), batch, seq_len, past_args)

    weights = (w_branch_attn[order, :].astype(BF16), w_branch_ret.astype(BF16), w_out.astype(BF16),
               g_mix_post.reshape(1, d_model), g_ffn_pre.reshape(1, d_model),
               w_ffn_gate.astype(BF16), w_ffn_up.astype(BF16), w_ffn_down.astype(BF16),
               g_ffn_post.reshape(1, d_model), w_ple_proj.astype(BF16), w_ple_gate.astype(BF16))
    y = _output_stage(oa, orr, gates, x2, p2, weights).reshape(batch, seq_len, d_model)

    tail = kv32.shape[0] // batch
    kv32 = kv32.reshape(batch, tail, 2, ATTN_KV_HEADS, ATTN_HEAD_DIM)
    k_new, v_new = kv32[:, :, 0], kv32[:, :, 1]
    if past is not None:
        n_win = past[0].shape[1]
        k_new = jnp.concatenate([past[0], k_new], axis=1)[:, -n_win:]
        v_new = jnp.concatenate([past[1], v_new], axis=1)[:, -n_win:]
    return y, k_new, v_new, state


def kernel(x_prompt, x_sample, p_prompt, p_sample, cache_attn_k, cache_attn_v, state_ret, g_mix_pre, w_in, attn_sinks, w_branch_attn, w_branch_ret, w_out, g_mix_post, g_ffn_pre, w_ffn_gate, w_ffn_up, w_ffn_down, g_ffn_post, w_ple_proj, w_ple_gate):
    depth = w_in.shape[0]
    assert cache_attn_k.shape[2] == WINDOW, "the rolling window must be full"
    pos_prompt = jnp.arange(x_prompt.shape[1])
    pos_sample = PAST_LEN + jnp.arange(x_sample.shape[1])
    y_p, y_s = x_prompt, x_sample
    outs = [[] for _ in range(6)]
    for i in range(depth):
        w_i = (g_mix_pre[i], w_in[i], attn_sinks[i], w_branch_attn[i], w_branch_ret[i], w_out[i],
               g_mix_post[i], g_ffn_pre[i], w_ffn_gate[i], w_ffn_up[i], w_ffn_down[i], g_ffn_post[i],
               w_ple_proj[i], w_ple_gate[i])
        y_p, kp, vp, rp = _layer(y_p, p_prompt[i], pos_prompt, None, w_i)
        y_s, ks, vs, rs = _layer(y_s, p_sample[i], pos_sample,
                                 (cache_attn_k[i], cache_attn_v[i], state_ret[i]), w_i)
        for lst, val in zip(outs, (kp, vp, rp.astype(x_prompt.dtype), ks, vs, rs.astype(x_sample.dtype))):
            lst.append(val)
    return (y_p, y_s) + tuple(jnp.stack(lst) for lst in outs)
```

```python
import functools

import jax
import jax.numpy as jnp
import numpy as np
from jax import lax
from jax.experimental import pallas as pl
from jax.experimental.pallas import tpu as pltpu

CHUNK = 64
WINDOW = 128
ATTN_HEADS = 8
ATTN_KV_HEADS = 2
ATTN_GROUP = ATTN_HEADS // ATTN_KV_HEADS
ATTN_HEAD_DIM = 64
RET_HEADS = 4
RET_KEY_DIM = 128
RET_VALUE_DIM = 256
RET_ROPE_BASE = 10000.0
NORM_EPS = 1e-6
GN_EPS = 1e-5
PAST_LEN = 2048

Q_A = ATTN_HEADS * ATTN_HEAD_DIM
KV_A = ATTN_KV_HEADS * ATTN_HEAD_DIM
QK_R = RET_HEADS * RET_KEY_DIM
V_R = RET_HEADS * RET_VALUE_DIM

QA_OFF = 0
KA_OFF = QA_OFF + Q_A
VA_OFF = KA_OFF + KV_A
QR_OFF = VA_OFF + KV_A
KR_OFF = QR_OFF + QK_R
VR_OFF = KR_OFF + QK_R
GR_OFF = VR_OFF + V_R
Z_WIDTH = GR_OFF + V_R

VMEM_LIMIT_BYTES = 56 * 1024 * 1024
TOKEN_TILE = 512
MIXER_TILE = 256
RET_BLOCK = 256
ATT_GROUP = 128
FFN_CHUNK = 512

BF16 = jnp.bfloat16
F32 = jnp.float32


def _dot(a, b):
    return jnp.dot(a, b, preferred_element_type=F32)


def _dot_nt(a, b):
    return lax.dot_general(a, b, (((1,), (1,)), ((), ())), preferred_element_type=F32)


def _dot_tn(a, b):
    return lax.dot_general(a, b, (((0,), (0,)), ((), ())), preferred_element_type=F32)


def _sigmoid(x):
    return 1.0 / (1.0 + jnp.exp(-x))


def _rmsnorm(x, g):
    return x * lax.rsqrt(jnp.mean(x * x, axis=-1, keepdims=True) + NORM_EPS) * g


def _resident(shape):
    return pl.BlockSpec(shape, lambda *_: (0,) * len(shape), pipeline_mode=pl.Buffered(1))


def _project(x_ref, g_ref, w_ref, cos_ref, sin_ref, z_ref, gates_ref):
    h = _rmsnorm(x_ref[...], g_ref[...]).astype(BF16)

    z_ref[:, QA_OFF:QA_OFF + Q_A] = (_dot(h, w_ref[:, QA_OFF:QA_OFF + Q_A]) * (ATTN_HEAD_DIM ** -0.5)).astype(BF16)
    kv = _dot(h, w_ref[:, KA_OFF:KA_OFF + 2 * KV_A])
    z_ref[:, KA_OFF:KA_OFF + 2 * KV_A] = kv.astype(BF16)

    cos = cos_ref[...]
    sin = sin_ref[...]
    for off, scale in ((QR_OFF, None), (KR_OFF, RET_KEY_DIM ** -0.5)):
        z = _dot(h, w_ref[:, off:off + QK_R])
        for hh in range(RET_HEADS):
            zh = z[:, hh * RET_KEY_DIM:(hh + 1) * RET_KEY_DIM]
            rot = zh * cos + pltpu.roll(zh, RET_KEY_DIM // 2, 1) * sin
            if scale is not None:
                rot = rot * scale
            z_ref[:, off + hh * RET_KEY_DIM:off + (hh + 1) * RET_KEY_DIM] = rot.astype(BF16)

    for off in (VR_OFF, GR_OFF):
        z_ref[:, off:off + V_R] = _dot(h, w_ref[:, off:off + V_R]).astype(BF16)
    d_model = x_ref.shape[1]
    for j in range(2):
        gates_ref[:, j * d_model:(j + 1) * d_model] = _dot(
            h, w_ref[:, Z_WIDTH + j * d_model:Z_WIDTH + (j + 1) * d_model]).astype(BF16)
    return kv


def _mix(z_ref, sink_ref, bias_ref, dec_ref, cross_ref, kdec_ref, oa_ref, or_ref, kbuf, vbuf, state,
         *, tile, ret_block, att_group, use_bias, seq_start, state_decay):
    kbuf[WINDOW:, :] = z_ref[:, KA_OFF:KA_OFF + KV_A]
    vbuf[WINDOW:, :] = z_ref[:, VA_OFF:VA_OFF + KV_A]

    low_half = lax.broadcasted_iota(jnp.int32, (att_group, 2 * ATTN_HEAD_DIM), 1) < ATTN_HEAD_DIM
    win = WINDOW + att_group
    for gi in range(tile // att_group):
        rows = slice(gi * att_group, (gi + 1) * att_group)
        q_tiles = [z_ref[rows, QA_OFF + j * 128:QA_OFF + (j + 1) * 128] for j in range(ATTN_GROUP)]
        zero = jnp.zeros_like(q_tiles[0])
        lhs = jnp.concatenate([jnp.where(low_half, q, zero) for q in q_tiles]
                              + [jnp.where(low_half, zero, q) for q in q_tiles], axis=0)
        k_win = kbuf[gi * att_group:gi * att_group + win, :]
        v_win = vbuf[gi * att_group:gi * att_group + win, :]
        s_all = _dot_nt(lhs, k_win)
        if use_bias:
            bias = bias_ref[0] if (seq_start is None or gi > 0) else bias_ref[jnp.where(seq_start, 1, 0)]
        probs, inv_denoms = [], []
        for r in range(ATTN_HEADS):
            s = s_all[r * att_group:(r + 1) * att_group]
            if use_bias:
                s = s + bias
            sink = sink_ref[r]
            m = jnp.maximum(jnp.max(s, axis=1, keepdims=True), sink)
            p = jnp.exp(s - m)
            inv_denoms.append(1.0 / (jnp.sum(p, axis=1, keepdims=True) + jnp.exp(sink - m)))
            probs.append(p.astype(BF16))
        o = _dot(jnp.concatenate(probs, axis=0), v_win)
        for j in range(ATTN_GROUP):
            r_lo, r_hi = j, ATTN_GROUP + j
            lo = o[r_lo * att_group:(r_lo + 1) * att_group] * inv_denoms[r_lo]
            hi = o[r_hi * att_group:(r_hi + 1) * att_group] * inv_denoms[r_hi]
            oa_ref[rows, j * 128:(j + 1) * 128] = jnp.where(low_half, lo, hi).astype(BF16)

    for bi in range(tile // ret_block):
        rows = slice(bi * ret_block, (bi + 1) * ret_block)
        for hh in range(RET_HEADS):
            vcols = slice(hh * RET_VALUE_DIM, (hh + 1) * RET_VALUE_DIM)
            q = z_ref[rows, QR_OFF + hh * RET_KEY_DIM:QR_OFF + (hh + 1) * RET_KEY_DIM]
            k = z_ref[rows, KR_OFF + hh * RET_KEY_DIM:KR_OFF + (hh + 1) * RET_KEY_DIM]
            v = z_ref[rows, VR_OFF + hh * RET_VALUE_DIM:VR_OFF + (hh + 1) * RET_VALUE_DIM]
            st = state[hh]
            intra = _dot((_dot_nt(q, k) * dec_ref[hh]).astype(BF16), v)
            cross = _dot(q, st.astype(BF16)) * cross_ref[hh]
            kd = (k.astype(F32) * kdec_ref[hh]).astype(BF16)
            state[hh] = state_decay[hh] * st + _dot_tn(kd, v)
            o_r = intra + cross
            mu = jnp.mean(o_r, axis=1, keepdims=True)
            cen = o_r - mu
            var = jnp.mean(cen * cen, axis=1, keepdims=True)
            g = z_ref[rows, GR_OFF + hh * RET_VALUE_DIM:GR_OFF + (hh + 1) * RET_VALUE_DIM].astype(F32)
            or_ref[rows, vcols] = (cen * lax.rsqrt(var + GN_EPS) * (g * _sigmoid(g))).astype(BF16)

    kbuf[0:WINDOW, :] = kbuf[tile:tile + WINDOW, :]
    vbuf[0:WINDOW, :] = vbuf[tile:tile + WINDOW, :]


def _mix_config(tile):
    ret_block = min(RET_BLOCK, tile)
    att_group = min(ATT_GROUP, tile)
    assert tile % ret_block == 0 and tile % att_group == 0 and att_group % CHUNK == 0
    tables = (_attention_bias(att_group),) + _retention_tables(ret_block)
    return dict(tile=tile, ret_block=ret_block, att_group=att_group, state_decay=_state_decay(ret_block)), tables


STATE_SHAPE = (RET_HEADS, RET_KEY_DIM, RET_VALUE_DIM)


def _front_kernel(x_ref, g_ref, w_ref, cos_ref, sin_ref, sink_ref, bias_ref, dec_ref, cross_ref, kdec_ref,
                  oa_ref, or_ref, gates_ref, kv32_ref, st_out_ref,
                  z_even, z_odd, kbuf, vbuf, state, *, tiles_per_seq, tail_rows, mix_cfg):
    i = pl.program_id(0)
    tile = mix_cfg["tile"]

    @pl.when(i == 0)
    def _():
        z_odd[...] = jnp.zeros(z_odd.shape, BF16)
        kbuf[0:WINDOW, :] = jnp.zeros((WINDOW, KV_A), BF16)
        vbuf[0:WINDOW, :] = jnp.zeros((WINDOW, KV_A), BF16)
        state[...] = jnp.zeros(state.shape, F32)

    def step(z_write, z_read):
        seq_start = (jnp.maximum(i - 1, 0) % tiles_per_seq) == 0
        kbuf[0:WINDOW, :] = jnp.where(seq_start, jnp.zeros((WINDOW, KV_A), BF16), kbuf[0:WINDOW, :])
        vbuf[0:WINDOW, :] = jnp.where(seq_start, jnp.zeros((WINDOW, KV_A), BF16), vbuf[0:WINDOW, :])
        state[...] = jnp.where(seq_start, jnp.zeros(state.shape, F32), state[...])
        _mix(z_read, sink_ref, bias_ref, dec_ref, cross_ref, kdec_ref, oa_ref, or_ref, kbuf, vbuf, state,
             use_bias=True, seq_start=seq_start, **mix_cfg)
        st_out_ref[0] = state[...]

        kv = _project(x_ref, g_ref, w_ref, cos_ref, sin_ref, z_write, gates_ref)
        kv32_ref[...] = kv[tile - tail_rows:, :]

    pl.when(i % 2 == 0)(lambda: step(z_even, z_odd))
    pl.when(i % 2 == 1)(lambda: step(z_odd, z_even))


def _front_fused(x2, g_pre, w_in_bf16, cos_tab, sin_tab, sinks, batch, seq_len):
    n, d_model = x2.shape
    d_in = w_in_bf16.shape[1]
    tile = TOKEN_TILE
    assert seq_len % tile == 0 and tile >= WINDOW
    tiles_per_seq = seq_len // tile
    n_tiles = n // tile
    tail_rows = WINDOW
    mix_cfg, tables = _mix_config(tile)

    def proj_tile(i):
        return jnp.minimum(i, n_tiles - 1)

    def mix_tile(i):
        return jnp.maximum(i - 1, 0)

    def mixed_rows(width):
        return pl.BlockSpec((tile, width), lambda i: (mix_tile(i), 0))

    in_specs = [pl.BlockSpec((tile, d_model), lambda i: (proj_tile(i), 0)),
                _resident((1, d_model)), _resident((d_model, d_in)),
                pl.BlockSpec((tile, RET_KEY_DIM), lambda i: (proj_tile(i) % tiles_per_seq, 0)),
                pl.BlockSpec((tile, RET_KEY_DIM), lambda i: (proj_tile(i) % tiles_per_seq, 0)),
                pl.BlockSpec(memory_space=pltpu.SMEM)]
    in_specs += [_resident(tab.shape) for tab in tables]
    out_specs = [mixed_rows(Q_A), mixed_rows(V_R),
                 pl.BlockSpec((tile, 2 * d_model), lambda i: (proj_tile(i), 0)),
                 pl.BlockSpec((tail_rows, 2 * KV_A), lambda i: (proj_tile(i) // tiles_per_seq, 0)),
                 pl.BlockSpec((1,) + STATE_SHAPE, lambda i: (mix_tile(i) // tiles_per_seq, 0, 0, 0))]
    out_shape = [jax.ShapeDtypeStruct((n, Q_A), BF16), jax.ShapeDtypeStruct((n, V_R), BF16),
                 jax.ShapeDtypeStruct((n, 2 * d_model), BF16),
                 jax.ShapeDtypeStruct((batch * tail_rows, 2 * KV_A), F32),
                 jax.ShapeDtypeStruct((batch,) + STATE_SHAPE, F32)]
    return pl.pallas_call(
        functools.partial(_front_kernel, tiles_per_seq=tiles_per_seq, tail_rows=tail_rows, mix_cfg=mix_cfg),
        grid=(n_tiles + 1,),
        in_specs=in_specs,
        out_specs=out_specs,
        out_shape=out_shape,
        scratch_shapes=[pltpu.VMEM((tile, Z_WIDTH), BF16), pltpu.VMEM((tile, Z_WIDTH), BF16),
                        pltpu.VMEM((WINDOW + tile, KV_A), BF16), pltpu.VMEM((WINDOW + tile, KV_A), BF16),
                        pltpu.VMEM(STATE_SHAPE, F32)],
        compiler_params=pltpu.CompilerParams(
            dimension_semantics=("arbitrary",), vmem_limit_bytes=VMEM_LIMIT_BYTES),
        name="front",
    )(x2, g_pre, w_in_bf16, cos_tab, sin_tab, sinks, *tables)


def _inproj_kernel(x_ref, g_ref, w_ref, cos_ref, sin_ref, z_ref, gates_ref, kv32_ref):
    kv32_ref[...] = _project(x_ref, g_ref, w_ref, cos_ref, sin_ref, z_ref, gates_ref)


def _in_projection(x2, g_pre, w_in_bf16, cos_tab, sin_tab, seq_len):
    n, d_model = x2.shape
    d_in = w_in_bf16.shape[1]
    tile = min(TOKEN_TILE, n)
    assert n % tile == 0 and tile % seq_len == 0 and seq_len <= WINDOW
    cos_tab = jnp.tile(cos_tab, (tile // seq_len, 1))
    sin_tab = jnp.tile(sin_tab, (tile // seq_len, 1))

    def row_spec(width):
        return pl.BlockSpec((tile, width), lambda i: (i, 0))

    tab_spec = pl.BlockSpec((tile, RET_KEY_DIM), lambda i: (0, 0))
    out_widths = (Z_WIDTH, 2 * d_model)
    out_shape = [jax.ShapeDtypeStruct((n, w), BF16) for w in out_widths]
    out_shape.append(jax.ShapeDtypeStruct((n, 2 * KV_A), F32))
    return pl.pallas_call(
        _inproj_kernel,
        grid=(n // tile,),
        in_specs=[row_spec(d_model), _resident((1, d_model)), _resident((d_model, d_in)), tab_spec, tab_spec],
        out_specs=[row_spec(w) for w in out_widths] + [row_spec(2 * KV_A)],
        out_shape=out_shape,
        compiler_params=pltpu.CompilerParams(
            dimension_semantics=("arbitrary",), vmem_limit_bytes=VMEM_LIMIT_BYTES),
        name="in_projection",
    )(x2, g_pre, w_in_bf16, cos_tab, sin_tab)


def _mixer_kernel(*refs, has_past, use_bias, mix_cfg):
    z_ref, sink_ref, bias_ref, dec_ref, cross_ref, kdec_ref = refs[:6]
    rest = refs[6:]
    if has_past:
        ck_ref, cv_ref, st0_ref = rest[:3]
        rest = rest[3:]
    oa_ref, or_ref, st_out_ref, kbuf, vbuf, state = rest
    t = pl.program_id(1)

    @pl.when(t == 0)
    def _():
        if has_past:
            kbuf[0:WINDOW, :] = ck_ref[0]
            vbuf[0:WINDOW, :] = cv_ref[0]
            state[...] = st0_ref[0]
        else:
            kbuf[0:WINDOW, :] = jnp.zeros((WINDOW, KV_A), BF16)
            vbuf[0:WINDOW, :] = jnp.zeros((WINDOW, KV_A), BF16)
            state[...] = jnp.zeros(state.shape, F32)

    _mix(z_ref, sink_ref, bias_ref, dec_ref, cross_ref, kdec_ref, oa_ref, or_ref, kbuf, vbuf, state,
         use_bias=use_bias, seq_start=None if has_past else t == 0, **mix_cfg)

    @pl.when(t == pl.num_programs(1) - 1)
    def _():
        st_out_ref[0] = state[...]


def _mixers(z, sinks, batch, seq_len, past=None):
    tile = min(MIXER_TILE, seq_len)
    assert seq_len % tile == 0
    nt = seq_len // tile
    n = batch * seq_len
    mix_cfg, tables = _mix_config(tile)
    use_bias = not (past is not None and mix_cfg["att_group"] == CHUNK)

    def row_spec(width):
        return pl.BlockSpec((tile, width), lambda b, t: (b * nt + t, 0))

    in_specs = [row_spec(Z_WIDTH), pl.BlockSpec(memory_space=pltpu.SMEM)]
    in_specs += [_resident(tab.shape) for tab in tables]
    args = [z, sinks] + list(tables)
    if past is not None:
        in_specs += [pl.BlockSpec((1, WINDOW, KV_A), lambda b, t: (b, 0, 0)),
                     pl.BlockSpec((1, WINDOW, KV_A), lambda b, t: (b, 0, 0)),
                     pl.BlockSpec((1,) + STATE_SHAPE, lambda b, t: (b, 0, 0, 0))]
        args += list(past)
    return pl.pallas_call(
        functools.partial(_mixer_kernel, has_past=past is not None, use_bias=use_bias, mix_cfg=mix_cfg),
        grid=(batch, nt),
        in_specs=in_specs,
        out_specs=[row_spec(Q_A), row_spec(V_R),
                   pl.BlockSpec((1,) + STATE_SHAPE, lambda b, t: (b, 0, 0, 0))],
        out_shape=[jax.ShapeDtypeStruct((n, Q_A), BF16), jax.ShapeDtypeStruct((n, V_R), BF16),
                   jax.ShapeDtypeStruct((batch,) + STATE_SHAPE, F32)],
        scratch_shapes=[pltpu.VMEM((WINDOW + tile, KV_A), BF16), pltpu.VMEM((WINDOW + tile, KV_A), BF16),
                        pltpu.VMEM(STATE_SHAPE, F32)],
        compiler_params=pltpu.CompilerParams(
            dimension_semantics=("arbitrary", "arbitrary"), vmem_limit_bytes=VMEM_LIMIT_BYTES),
        name="mixers",
    )(*args)


def _post_kernel(oa_ref, or_ref, gates_ref, x_ref, p_ref,
                 wba_ref, wbr_ref, wout_ref, gpost_ref, gfpre_ref, wg_ref, wu_ref, wd_ref, gfpost_ref,
                 wpp_ref, wpg_ref, y_ref, *, ffn_slabs):
    d_model = x_ref.shape[1]
    gate_a = gates_ref[:, :d_model].astype(F32)
    gate_r = gates_ref[:, d_model:].astype(F32)
    merged = (_sigmoid(gate_a) * _dot(oa_ref[...], wba_ref[...])
              + _sigmoid(gate_r) * _dot(or_ref[...], wbr_ref[...]))
    y = x_ref[...] + _rmsnorm(_dot(merged.astype(BF16), wout_ref[...]), gpost_ref[...])

    h = _rmsnorm(y, gfpre_ref[...]).astype(BF16)
    f = None
    for lo, hi in ffn_slabs:
        gate = _dot(h, wg_ref[:, lo:hi])
        act = (gate * _sigmoid(gate) * _dot(h, wu_ref[:, lo:hi])).astype(BF16)
        part = _dot(act, wd_ref[lo:hi, :])
        f = part if f is None else f + part
    y = y + _rmsnorm(f, gfpost_ref[...])

    emb = _dot(p_ref[...].astype(BF16), wpp_ref[...])
    y_ref[...] = y + emb * _sigmoid(_dot(y.astype(BF16), wpg_ref[...]))


def _output_stage(oa, orr, gates, x2, p2, weights):
    n, d_model = x2.shape
    tile = min(TOKEN_TILE, n)
    assert n % tile == 0
    ffn_hidden = weights[7].shape[0]
    ffn_slabs = tuple((lo, min(lo + FFN_CHUNK, ffn_hidden)) for lo in range(0, ffn_hidden, FFN_CHUNK))

    def row_spec(width):
        return pl.BlockSpec((tile, width), lambda i: (i, 0))

    acts = (oa, orr, gates, x2, p2)
    return pl.pallas_call(
        functools.partial(_post_kernel, ffn_slabs=ffn_slabs),
        grid=(n // tile,),
        in_specs=[row_spec(a.shape[1]) for a in acts] + [_resident(w.shape) for w in weights],
        out_specs=row_spec(d_model),
        out_shape=jax.ShapeDtypeStruct((n, d_model), F32),
        compiler_params=pltpu.CompilerParams(
            dimension_semantics=("arbitrary",), vmem_limit_bytes=VMEM_LIMIT_BYTES),
        name="output_stage",
    )(*acts, *weights)


def _rotary_tables(pos):
    half = RET_KEY_DIM // 2
    inv = 1.0 / (RET_ROPE_BASE ** jnp.linspace(0.0, 1.0, half, dtype=F32))
    ang = pos.astype(F32)[:, None] * inv[None, :]
    cos, sin = jnp.cos(ang), jnp.sin(ang)
    return jnp.concatenate([cos, cos], axis=1), jnp.concatenate([-sin, sin], axis=1)


def _retention_tables(block):
    log_g = jnp.log1p(-jnp.exp2(-5.0 - jnp.arange(RET_HEADS, dtype=F32)))
    idx = jnp.arange(block, dtype=F32)
    diff = idx[:, None] - idx[None, :]
    decay = jnp.where(diff >= 0, jnp.exp(log_g[:, None, None] * jnp.maximum(diff, 0.0)), 0.0)
    cross = jnp.exp(log_g[:, None] * (idx[None, :] + 1.0))
    k_dec = jnp.exp(log_g[:, None] * (float(block) - 1.0 - idx[None, :]))
    cross = jnp.broadcast_to(cross[:, :, None], (RET_HEADS, block, RET_VALUE_DIM))
    k_dec = jnp.broadcast_to(k_dec[:, :, None], (RET_HEADS, block, RET_KEY_DIM))
    return decay, cross, k_dec


def _state_decay(block):
    log_g = np.log1p(-np.exp2(-5.0 - np.arange(RET_HEADS, dtype=np.float64)))
    return tuple(float(v) for v in np.exp(log_g * block))


def _attention_bias(group):
    q_chunk = np.arange(group)[:, None] // CHUNK
    k_chunk = np.arange(WINDOW + group)[None, :] // CHUNK
    visible = (k_chunk >= q_chunk) & (k_chunk <= q_chunk + WINDOW // CHUNK)
    at_start = visible & (k_chunk >= WINDOW // CHUNK)
    return jnp.asarray(np.where(np.stack([visible, at_start]), 0.0, -np.inf), dtype=F32)


def _head_pair_order():
    order = []
    for j in range(ATTN_GROUP):
        for kvh in range(ATTN_KV_HEADS):
            head = kvh * ATTN_GROUP + j
            order.extend(range(head * ATTN_HEAD_DIM, (head + 1) * ATTN_HEAD_DIM))
    return np.asarray(order, dtype=np.int32)


def _layer(x, p, pos, past, w):
    (g_mix_pre, w_in, attn_sinks, w_branch_attn, w_branch_ret, w_out, g_mix_post, g_ffn_pre,
     w_ffn_gate, w_ffn_up, w_ffn_down, g_ffn_post, w_ple_proj, w_ple_gate) = w
    batch, seq_len, d_model = x.shape
    n = batch * seq_len
    x2 = x.reshape(n, d_model)
    p2 = p.reshape(n, p.shape[-1])

    order = _head_pair_order()
    w_in_b = jnp.concatenate([w_in[:, :Q_A][:, order], w_in[:, Q_A:]], axis=1).astype(BF16)
    cos_tab, sin_tab = _rotary_tables(pos)
    g_pre = g_mix_pre.reshape(1, d_model)
    sinks = attn_sinks.astype(F32)
    if past is None and seq_len >= TOKEN_TILE:
        oa, orr, gates, kv32, state = _front_fused(x2, g_pre, w_in_b, cos_tab, sin_tab, sinks, batch, seq_len)
    else:
        z, gates, kv32 = _in_projection(x2, g_pre, w_in_b, cos_tab, sin_tab, seq_len)
        past_args = None
        if past is not None:
            cache_k, cache_v, state0 = past
            past_args = (cache_k.reshape(batch, WINDOW, KV_A).astype(BF16),
                         cache_v.reshape(batch, WINDOW, KV_A).astype(BF16), state0.astype(F32))
        oa, orr, state = _mixers(z, sinks, batch, seq_len, past_args)

    weights = (w_branch_attn[order, :].astype(BF16), w_branch_ret.astype(BF16), w_out.astype(BF16),
               g_mix_post.reshape(1, d_model), g_ffn_pre.reshape(1, d_model),
               w_ffn_gate.astype(BF16), w_ffn_up.astype(BF16), w_ffn_down.astype(BF16),
               g_ffn_post.reshape(1, d_model), w_ple_proj.astype(BF16), w_ple_gate.astype(BF16))
    y = _output_stage(oa, orr, gates, x2, p2, weights).reshape(batch, seq_len, d_model)

    tail = kv32.shape[0] // batch
    kv32 = kv32.reshape(batch, tail, 2, ATTN_KV_HEADS, ATTN_HEAD_DIM)
    k_new, v_new = kv32[:, :, 0], kv32[:, :, 1]
    if past is not None:
        n_win = past[0].shape[1]
        k_new = jnp.concatenate([past[0], k_new], axis=1)[:, -n_win:]
        v_new = jnp.concatenate([past[1], v_new], axis=1)[:, -n_win:]
    return y, k_new, v_new, state


def kernel(x_prompt, x_sample, p_prompt, p_sample, cache_attn_k, cache_attn_v, state_ret, g_mix_pre, w_in, attn_sinks, w_branch_attn, w_branch_ret, w_out, g_mix_post, g_ffn_pre, w_ffn_gate, w_ffn_up, w_ffn_down, g_ffn_post, w_ple_proj, w_ple_gate):
    depth = w_in.shape[0]
    assert cache_attn_k.shape[2] == WINDOW, "the rolling window must be full"
    pos_prompt = jnp.arange(x_prompt.shape[1])
    pos_sample = PAST_LEN + jnp.arange(x_sample.shape[1])
    y_p, y_s = x_prompt, x_sample
    outs = [[] for _ in range(6)]
    for i in range(depth):
        w_i = (g_mix_pre[i], w_in[i], attn_sinks[i], w_branch_attn[i], w_branch_ret[i], w_out[i],
               g_mix_post[i], g_ffn_pre[i], w_ffn_gate[i], w_ffn_up[i], w_ffn_down[i], g_ffn_post[i],
               w_ple_proj[i], w_ple_gate[i])
        y_p, kp, vp, rp = _layer(y_p, p_prompt[i], pos_prompt, None, w_i)
        y_s, ks, vs, rs = _layer(y_s, p_sample[i], pos_sample,
                                 (cache_attn_k[i], cache_attn_v[i], state_ret[i]), w_i)
        for lst, val in zip(outs, (kp, vp, rp.astype(x_prompt.dtype), ks, vs, rs.astype(x_sample.dtype))):
            lst.append(val)
    return (y_p, y_s) + tuple(jnp.stack(lst) for lst in outs)
```

```python
import functools

import jax
import jax.numpy as jnp
import numpy as np
from jax import lax
from jax.experimental import pallas as pl
from jax.experimental.pallas import tpu as pltpu

CHUNK = 64
WINDOW = 128
ATTN_HEADS = 8
ATTN_KV_HEADS = 2
ATTN_GROUP = ATTN_HEADS // ATTN_KV_HEADS
ATTN_HEAD_DIM = 64
RET_HEADS = 4
RET_KEY_DIM = 128
RET_VALUE_DIM = 256
RET_ROPE_BASE = 10000.0
NORM_EPS = 1e-6
GN_EPS = 1e-5
PAST_LEN = 2048

Q_A = ATTN_HEADS * ATTN_HEAD_DIM
KV_A = ATTN_KV_HEADS * ATTN_HEAD_DIM
QK_R = RET_HEADS * RET_KEY_DIM
V_R = RET_HEADS * RET_VALUE_DIM

QA_OFF = 0
KA_OFF = QA_OFF + Q_A
VA_OFF = KA_OFF + KV_A
QR_OFF = VA_OFF + KV_A
KR_OFF = QR_OFF + QK_R
VR_OFF = KR_OFF + QK_R
GR_OFF = VR_OFF + V_R
Z_WIDTH = GR_OFF + V_R

VMEM_LIMIT_BYTES = 56 * 1024 * 1024
TOKEN_TILE = 512
MIXER_TILE = 256
RET_BLOCK = 256
ATT_GROUP = 128
FFN_CHUNK = 512

BF16 = jnp.bfloat16
F32 = jnp.float32
LOG2E = float(np.log2(np.e))


def _dot(a, b):
    return jnp.dot(a, b, preferred_element_type=F32)


def _dot_nt(a, b):
    return lax.dot_general(a, b, (((1,), (1,)), ((), ())), preferred_element_type=F32)


def _dot_tn(a, b):
    return lax.dot_general(a, b, (((0,), (0,)), ((), ())), preferred_element_type=F32)


def _sigmoid(x):
    return 1.0 / (1.0 + jnp.exp(-x))


def _rmsnorm(x, g):
    return x * lax.rsqrt(jnp.mean(x * x, axis=-1, keepdims=True) + NORM_EPS) * g


def _resident(shape):
    return pl.BlockSpec(shape, lambda *_: (0,) * len(shape), pipeline_mode=pl.Buffered(1))


def _project(x_ref, g_ref, w_ref, cos_ref, sin_ref, qdec_ref, kdec_ref, z_ref, gates_ref):
    h = _rmsnorm(x_ref[...], g_ref[...]).astype(BF16)

    z_ref[:, QA_OFF:QA_OFF + Q_A] = (_dot(h, w_ref[:, QA_OFF:QA_OFF + Q_A])
                                     * (ATTN_HEAD_DIM ** -0.5 * LOG2E)).astype(BF16)
    kv = _dot(h, w_ref[:, KA_OFF:KA_OFF + 2 * KV_A])
    z_ref[:, KA_OFF:KA_OFF + 2 * KV_A] = kv.astype(BF16)

    cos = cos_ref[...]
    sin = sin_ref[...]
    for off, dec_ref in ((QR_OFF, qdec_ref), (KR_OFF, kdec_ref)):
        z = _dot(h, w_ref[:, off:off + QK_R])
        for hh in range(RET_HEADS):
            zh = z[:, hh * RET_KEY_DIM:(hh + 1) * RET_KEY_DIM]
            rot = (zh * cos + pltpu.roll(zh, RET_KEY_DIM // 2, 1) * sin) * dec_ref[hh]
            z_ref[:, off + hh * RET_KEY_DIM:off + (hh + 1) * RET_KEY_DIM] = rot.astype(BF16)

    z_ref[:, VR_OFF:VR_OFF + V_R] = _dot(h, w_ref[:, VR_OFF:VR_OFF + V_R]).astype(BF16)
    g = _dot(h, w_ref[:, GR_OFF:GR_OFF + V_R])
    z_ref[:, GR_OFF:GR_OFF + V_R] = (g * _sigmoid(g)).astype(BF16)
    d_model = x_ref.shape[1]
    for j in range(2):
        gates_ref[:, j * d_model:(j + 1) * d_model] = _dot(
            h, w_ref[:, Z_WIDTH + j * d_model:Z_WIDTH + (j + 1) * d_model]).astype(BF16)
    return kv


def _mix(z_ref, sink_ref, bias_ref, tril_ref, oa_ref, or_ref, kbuf, vbuf, state,
         *, tile, ret_block, att_group, use_bias, seq_start, state_decay):
    kbuf[WINDOW:, :] = z_ref[:, KA_OFF:KA_OFF + KV_A]
    vbuf[WINDOW:, 0:KV_A] = z_ref[:, VA_OFF:VA_OFF + KV_A]

    low_half = lax.broadcasted_iota(jnp.int32, (att_group, 2 * ATTN_HEAD_DIM), 1) < ATTN_HEAD_DIM
    win = WINDOW + att_group
    for gi in range(tile // att_group):
        rows = slice(gi * att_group, (gi + 1) * att_group)
        q_tiles = [z_ref[rows, QA_OFF + j * 128:QA_OFF + (j + 1) * 128] for j in range(ATTN_GROUP)]
        zero = jnp.zeros_like(q_tiles[0])
        lhs = jnp.concatenate([jnp.where(low_half, q, zero) for q in q_tiles]
                              + [jnp.where(low_half, zero, q) for q in q_tiles], axis=0)
        k_win = kbuf[gi * att_group:gi * att_group + win, :]
        v_win = vbuf[gi * att_group:gi * att_group + win, :]
        s_all = _dot_nt(lhs, k_win)
        if use_bias:
            bias = bias_ref[0] if (seq_start is None or gi > 0) else bias_ref[jnp.where(seq_start, 1, 0)]
        probs, sink_terms = [], []
        for r in range(ATTN_HEADS):
            s = s_all[r * att_group:(r + 1) * att_group]
            if use_bias:
                s = s + bias
            sink = sink_ref[r] * LOG2E
            m = jnp.maximum(jnp.max(s, axis=1, keepdims=True), sink)
            sink_terms.append(jnp.exp2(sink - m))
            probs.append(jnp.exp2(s - m).astype(BF16))
        o = _dot(jnp.concatenate(probs, axis=0), v_win)

        def head_out(r):
            blk = o[r * att_group:(r + 1) * att_group]
            return blk[:, 0:KV_A] * (1.0 / (blk[:, KV_A:] + sink_terms[r]))

        for j in range(ATTN_GROUP):
            oa_ref[rows, j * 128:(j + 1) * 128] = jnp.where(
                low_half, head_out(j), head_out(ATTN_GROUP + j)).astype(BF16)

    for bi in range(tile // ret_block):
        rows = slice(bi * ret_block, (bi + 1) * ret_block)
        for hh in range(RET_HEADS):
            vcols = slice(hh * RET_VALUE_DIM, (hh + 1) * RET_VALUE_DIM)
            q = z_ref[rows, QR_OFF + hh * RET_KEY_DIM:QR_OFF + (hh + 1) * RET_KEY_DIM]
            k = z_ref[rows, KR_OFF + hh * RET_KEY_DIM:KR_OFF + (hh + 1) * RET_KEY_DIM]
            v = z_ref[rows, VR_OFF + hh * RET_VALUE_DIM:VR_OFF + (hh + 1) * RET_VALUE_DIM]
            st = state[hh]
            o_r = _dot((_dot_nt(q, k) * tril_ref[...]).astype(BF16), v) + _dot(q, st.astype(BF16))
            state[hh] = state_decay[hh] * (st + _dot_tn(k, v))
            mu = jnp.mean(o_r, axis=1, keepdims=True)
            cen = o_r - mu
            var = jnp.mean(cen * cen, axis=1, keepdims=True)
            gate = z_ref[rows, GR_OFF + hh * RET_VALUE_DIM:GR_OFF + (hh + 1) * RET_VALUE_DIM].astype(F32)
            or_ref[rows, vcols] = (cen * lax.rsqrt(var + GN_EPS) * gate).astype(BF16)

    kbuf[0:WINDOW, :] = kbuf[tile:tile + WINDOW, :]
    vbuf[0:WINDOW, 0:KV_A] = vbuf[tile:tile + WINDOW, 0:KV_A]


def _init_window(kbuf, vbuf, k_rows, v_rows):
    kbuf[0:WINDOW, :] = k_rows
    vbuf[0:WINDOW, 0:KV_A] = v_rows
    vbuf[:, KV_A:] = jnp.ones((vbuf.shape[0], KV_A), BF16)


def _mix_config(tile):
    ret_block = min(RET_BLOCK, tile)
    att_group = min(ATT_GROUP, tile)
    assert tile % ret_block == 0 and tile % att_group == 0 and att_group % CHUNK == 0
    tril = jnp.asarray(np.tril(np.ones((ret_block, ret_block))), dtype=F32)
    tables = (_attention_bias(att_group), tril)
    return dict(tile=tile, ret_block=ret_block, att_group=att_group, state_decay=_state_decay(ret_block)), tables


STATE_SHAPE = (RET_HEADS, RET_KEY_DIM, RET_VALUE_DIM)


def _front_kernel(x_ref, g_ref, w_ref, cos_ref, sin_ref, qdec_ref, kdec_ref, sink_ref, bias_ref, tril_ref,
                  oa_ref, or_ref, gates_ref, kv32_ref, st_out_ref,
                  z_even, z_odd, kbuf, vbuf, state, *, tiles_per_seq, tail_rows, mix_cfg):
    i = pl.program_id(0)
    tile = mix_cfg["tile"]
    no_rows = jnp.zeros((WINDOW, KV_A), BF16)

    @pl.when(i == 0)
    def _():
        z_odd[...] = jnp.zeros(z_odd.shape, BF16)
        _init_window(kbuf, vbuf, no_rows, no_rows)
        state[...] = jnp.zeros(state.shape, F32)

    def step(z_write, z_read):
        seq_start = (jnp.maximum(i - 1, 0) % tiles_per_seq) == 0
        kbuf[0:WINDOW, :] = jnp.where(seq_start, no_rows, kbuf[0:WINDOW, :])
        vbuf[0:WINDOW, 0:KV_A] = jnp.where(seq_start, no_rows, vbuf[0:WINDOW, 0:KV_A])
        state[...] = jnp.where(seq_start, jnp.zeros(state.shape, F32), state[...])
        _mix(z_read, sink_ref, bias_ref, tril_ref, oa_ref, or_ref, kbuf, vbuf, state,
             use_bias=True, seq_start=seq_start, **mix_cfg)
        st_out_ref[0] = state[...]

        kv = _project(x_ref, g_ref, w_ref, cos_ref, sin_ref, qdec_ref, kdec_ref, z_write, gates_ref)
        kv32_ref[...] = kv[tile - tail_rows:, :]

    pl.when(i % 2 == 0)(lambda: step(z_even, z_odd))
    pl.when(i % 2 == 1)(lambda: step(z_odd, z_even))


def _front_fused(x2, g_pre, w_in_bf16, cos_tab, sin_tab, sinks, batch, seq_len):
    n, d_model = x2.shape
    d_in = w_in_bf16.shape[1]
    tile = TOKEN_TILE
    assert seq_len % tile == 0 and tile >= WINDOW
    tiles_per_seq = seq_len // tile
    n_tiles = n // tile
    tail_rows = WINDOW
    mix_cfg, tables = _mix_config(tile)
    dec_tabs = _projection_decay(tile, mix_cfg["ret_block"])

    def proj_tile(i):
        return jnp.minimum(i, n_tiles - 1)

    def mix_tile(i):
        return jnp.maximum(i - 1, 0)

    def mixed_rows(width):
        return pl.BlockSpec((tile, width), lambda i: (mix_tile(i), 0))

    in_specs = [pl.BlockSpec((tile, d_model), lambda i: (proj_tile(i), 0)),
                _resident((1, d_model)), _resident((d_model, d_in)),
                pl.BlockSpec((tile, RET_KEY_DIM), lambda i: (proj_tile(i) % tiles_per_seq, 0)),
                pl.BlockSpec((tile, RET_KEY_DIM), lambda i: (proj_tile(i) % tiles_per_seq, 0)),
                _resident(dec_tabs[0].shape), _resident(dec_tabs[1].shape),
                pl.BlockSpec(memory_space=pltpu.SMEM)]
    in_specs += [_resident(tab.shape) for tab in tables]
    out_specs = [mixed_rows(Q_A), mixed_rows(V_R),
                 pl.BlockSpec((tile, 2 * d_model), lambda i: (proj_tile(i), 0)),
                 pl.BlockSpec((tail_rows, 2 * KV_A), lambda i: (proj_tile(i) // tiles_per_seq, 0)),
                 pl.BlockSpec((1,) + STATE_SHAPE, lambda i: (mix_tile(i) // tiles_per_seq, 0, 0, 0))]
    out_shape = [jax.ShapeDtypeStruct((n, Q_A), BF16), jax.ShapeDtypeStruct((n, V_R), BF16),
                 jax.ShapeDtypeStruct((n, 2 * d_model), BF16),
                 jax.ShapeDtypeStruct((batch * tail_rows, 2 * KV_A), F32),
                 jax.ShapeDtypeStruct((batch,) + STATE_SHAPE, F32)]
    return pl.pallas_call(
        functools.partial(_front_kernel, tiles_per_seq=tiles_per_seq, tail_rows=tail_rows, mix_cfg=mix_cfg),
        grid=(n_tiles + 1,),
        in_specs=in_specs,
        out_specs=out_specs,
        out_shape=out_shape,
        scratch_shapes=[pltpu.VMEM((tile, Z_WIDTH), BF16), pltpu.VMEM((tile, Z_WIDTH), BF16),
                        pltpu.VMEM((WINDOW + tile, KV_A), BF16), pltpu.VMEM((WINDOW + tile, 2 * KV_A), BF16),
                        pltpu.VMEM(STATE_SHAPE, F32)],
        compiler_params=pltpu.CompilerParams(
            dimension_semantics=("arbitrary",), vmem_limit_bytes=VMEM_LIMIT_BYTES),
        name="front",
    )(x2, g_pre, w_in_bf16, cos_tab, sin_tab, *dec_tabs, sinks, *tables)


def _inproj_kernel(x_ref, g_ref, w_ref, cos_ref, sin_ref, qdec_ref, kdec_ref, z_ref, gates_ref, kv32_ref):
    kv32_ref[...] = _project(x_ref, g_ref, w_ref, cos_ref, sin_ref, qdec_ref, kdec_ref, z_ref, gates_ref)


def _in_projection(x2, g_pre, w_in_bf16, cos_tab, sin_tab, seq_len, ret_block):
    n, d_model = x2.shape
    d_in = w_in_bf16.shape[1]
    tile = min(TOKEN_TILE, n)
    assert n % tile == 0 and tile % seq_len == 0 and seq_len <= WINDOW and seq_len % ret_block == 0
    cos_tab = jnp.tile(cos_tab, (tile // seq_len, 1))
    sin_tab = jnp.tile(sin_tab, (tile // seq_len, 1))
    dec_tabs = _projection_decay(tile, ret_block)

    def row_spec(width):
        return pl.BlockSpec((tile, width), lambda i: (i, 0))

    tab_spec = pl.BlockSpec((tile, RET_KEY_DIM), lambda i: (0, 0))
    out_widths = (Z_WIDTH, 2 * d_model)
    out_shape = [jax.ShapeDtypeStruct((n, w), BF16) for w in out_widths]
    out_shape.append(jax.ShapeDtypeStruct((n, 2 * KV_A), F32))
    return pl.pallas_call(
        _inproj_kernel,
        grid=(n // tile,),
        in_specs=[row_spec(d_model), _resident((1, d_model)), _resident((d_model, d_in)), tab_spec, tab_spec,
                  _resident(dec_tabs[0].shape), _resident(dec_tabs[1].shape)],
        out_specs=[row_spec(w) for w in out_widths] + [row_spec(2 * KV_A)],
        out_shape=out_shape,
        compiler_params=pltpu.CompilerParams(
            dimension_semantics=("arbitrary",), vmem_limit_bytes=VMEM_LIMIT_BYTES),
        name="in_projection",
    )(x2, g_pre, w_in_bf16, cos_tab, sin_tab, *dec_tabs)


def _mixer_kernel(*refs, has_past, use_bias, mix_cfg):
    z_ref, sink_ref, bias_ref, tril_ref = refs[:4]
    rest = refs[4:]
    if has_past:
        ck_ref, cv_ref, st0_ref = rest[:3]
        rest = rest[3:]
    oa_ref, or_ref, st_out_ref, kbuf, vbuf, state = rest
    t = pl.program_id(1)

    @pl.when(t == 0)
    def _():
        if has_past:
            _init_window(kbuf, vbuf, ck_ref[0], cv_ref[0])
            state[...] = st0_ref[0]
        else:
            _init_window(kbuf, vbuf, jnp.zeros((WINDOW, KV_A), BF16), jnp.zeros((WINDOW, KV_A), BF16))
            state[...] = jnp.zeros(state.shape, F32)

    _mix(z_ref, sink_ref, bias_ref, tril_ref, oa_ref, or_ref, kbuf, vbuf, state,
         use_bias=use_bias, seq_start=None if has_past else t == 0, **mix_cfg)

    @pl.when(t == pl.num_programs(1) - 1)
    def _():
        st_out_ref[0] = state[...]


def _mixers(z, sinks, batch, seq_len, past=None):
    tile = min(MIXER_TILE, seq_len)
    assert seq_len % tile == 0
    nt = seq_len // tile
    n = batch * seq_len
    mix_cfg, tables = _mix_config(tile)
    use_bias = not (past is not None and mix_cfg["att_group"] == CHUNK)

    def row_spec(width):
        return pl.BlockSpec((tile, width), lambda b, t: (b * nt + t, 0))

    in_specs = [row_spec(Z_WIDTH), pl.BlockSpec(memory_space=pltpu.SMEM)]
    in_specs += [_resident(tab.shape) for tab in tables]
    args = [z, sinks] + list(tables)
    if past is not None:
        in_specs += [pl.BlockSpec((1, WINDOW, KV_A), lambda b, t: (b, 0, 0)),
                     pl.BlockSpec((1, WINDOW, KV_A), lambda b, t: (b, 0, 0)),
                     pl.BlockSpec((1,) + STATE_SHAPE, lambda b, t: (b, 0, 0, 0))]
        args += list(past)
    return pl.pallas_call(
        functools.partial(_mixer_kernel, has_past=past is not None, use_bias=use_bias, mix_cfg=mix_cfg),
        grid=(batch, nt),
        in_specs=in_specs,
        out_specs=[row_spec(Q_A), row_spec(V_R),
                   pl.BlockSpec((1,) + STATE_SHAPE, lambda b, t: (b, 0, 0, 0))],
        out_shape=[jax.ShapeDtypeStruct((n, Q_A), BF16), jax.ShapeDtypeStruct((n, V_R), BF16),
                   jax.ShapeDtypeStruct((batch,) + STATE_SHAPE, F32)],
        scratch_shapes=[pltpu.VMEM((WINDOW + tile, KV_A), BF16), pltpu.VMEM((WINDOW + tile, 2 * KV_A), BF16),
                        pltpu.VMEM(STATE_SHAPE, F32)],
        compiler_params=pltpu.CompilerParams(
            dimension_semantics=("arbitrary", "arbitrary"), vmem_limit_bytes=VMEM_LIMIT_BYTES),
        name="mixers",
    )(*args)


def _post_kernel(oa_ref, or_ref, gates_ref, x_ref, p_ref,
                 wba_ref, wbr_ref, wout_ref, gpost_ref, gfpre_ref, wg_ref, wu_ref, wd_ref, gfpost_ref,
                 wpp_ref, wpg_ref, y_ref, *, ffn_slabs):
    d_model = x_ref.shape[1]
    gate_a = gates_ref[:, :d_model].astype(F32)
    gate_r = gates_ref[:, d_model:].astype(F32)
    merged = (_sigmoid(gate_a) * _dot(oa_ref[...], wba_ref[...])
              + _sigmoid(gate_r) * _dot(or_ref[...], wbr_ref[...]))
    y = x_ref[...] + _rmsnorm(_dot(merged.astype(BF16), wout_ref[...]), gpost_ref[...])

    h = _rmsnorm(y, gfpre_ref[...]).astype(BF16)
    f = None
    for lo, hi in ffn_slabs:
        gate = _dot(h, wg_ref[:, lo:hi])
        act = (gate * _sigmoid(gate) * _dot(h, wu_ref[:, lo:hi])).astype(BF16)
        part = _dot(act, wd_ref[lo:hi, :])
        f = part if f is None else f + part
    y = y + _rmsnorm(f, gfpost_ref[...])

    emb = _dot(p_ref[...].astype(BF16), wpp_ref[...])
    y_ref[...] = y + emb * _sigmoid(_dot(y.astype(BF16), wpg_ref[...]))


def _output_stage(oa, orr, gates, x2, p2, weights):
    n, d_model = x2.shape
    tile = min(TOKEN_TILE, n)
    assert n % tile == 0
    ffn_hidden = weights[7].shape[0]
    ffn_slabs = tuple((lo, min(lo + FFN_CHUNK, ffn_hidden)) for lo in range(0, ffn_hidden, FFN_CHUNK))

    def row_spec(width):
        return pl.BlockSpec((tile, width), lambda i: (i, 0))

    acts = (oa, orr, gates, x2, p2)
    return pl.pallas_call(
        functools.partial(_post_kernel, ffn_slabs=ffn_slabs),
        grid=(n // tile,),
        in_specs=[row_spec(a.shape[1]) for a in acts] + [_resident(w.shape) for w in weights],
        out_specs=row_spec(d_model),
        out_shape=jax.ShapeDtypeStruct((n, d_model), F32),
        compiler_params=pltpu.CompilerParams(
            dimension_semantics=("arbitrary",), vmem_limit_bytes=VMEM_LIMIT_BYTES),
        name="output_stage",
    )(*acts, *weights)


def _rotary_tables(pos):
    half = RET_KEY_DIM // 2
    inv = 1.0 / (RET_ROPE_BASE ** jnp.linspace(0.0, 1.0, half, dtype=F32))
    ang = pos.astype(F32)[:, None] * inv[None, :]
    cos, sin = jnp.cos(ang), jnp.sin(ang)
    return jnp.concatenate([cos, cos], axis=1), jnp.concatenate([-sin, sin], axis=1)


def _projection_decay(tile, block):
    log_g = jnp.log1p(-jnp.exp2(-5.0 - jnp.arange(RET_HEADS, dtype=F32)))
    pos = (jnp.arange(tile) % block).astype(F32) + 1.0
    q_dec = jnp.exp(log_g[:, None] * pos[None, :])
    k_dec = jnp.exp(-log_g[:, None] * pos[None, :]) * (RET_KEY_DIM ** -0.5)
    shape = (RET_HEADS, tile, RET_KEY_DIM)
    return jnp.broadcast_to(q_dec[:, :, None], shape), jnp.broadcast_to(k_dec[:, :, None], shape)


def _state_decay(block):
    log_g = np.log1p(-np.exp2(-5.0 - np.arange(RET_HEADS, dtype=np.float64)))
    return tuple(float(v) for v in np.exp(log_g * block))


def _attention_bias(group):
    q_chunk = np.arange(group)[:, None] // CHUNK
    k_chunk = np.arange(WINDOW + group)[None, :] // CHUNK
    visible = (k_chunk >= q_chunk) & (k_chunk <= q_chunk + WINDOW // CHUNK)
    at_start = visible & (k_chunk >= WINDOW // CHUNK)
    return jnp.asarray(np.where(np.stack([visible, at_start]), 0.0, -np.inf), dtype=F32)


def _head_pair_order():
    order = []
    for j in range(ATTN_GROUP):
        for kvh in range(ATTN_KV_HEADS):
            head = kvh * ATTN_GROUP + j
            order.extend(range(head * ATTN_HEAD_DIM, (head + 1) * ATTN_HEAD_DIM))
    return np.asarray(order, dtype=np.int32)


def _layer(x, p, pos, past, w):
    (g_mix_pre, w_in, attn_sinks, w_branch_attn, w_branch_ret, w_out, g_mix_post, g_ffn_pre,
     w_ffn_gate, w_ffn_up, w_ffn_down, g_ffn_post, w_ple_proj, w_ple_gate) = w
    batch, seq_len, d_model = x.shape
    n = batch * seq_len
    x2 = x.reshape(n, d_model)
    p2 = p.reshape(n, p.shape[-1])

    order = _head_pair_order()
    w_in_b = jnp.concatenate([w_in[:, :Q_A][:, order], w_in[:, Q_A:]], axis=1).astype(BF16)
    cos_tab, sin_tab = _rotary_tables(pos)
    g_pre = g_mix_pre.reshape(1, d_model)
    sinks = attn_sinks.astype(F32)
    if past is None and seq_len >= TOKEN_TILE:
        oa, orr, gates, kv32, state = _front_fused(x2, g_pre, w_in_b, cos_tab, sin_tab, sinks, batch, seq_len)
    else:
        ret_block = min(RET_BLOCK, MIXER_TILE, seq_len)
        z, gates, kv32 = _in_projection(x2, g_pre, w_in_b, cos_tab, sin_tab, seq_len, ret_block)
        past_args = None
        if past is not None:
            cache_k, cache_v, state0 = past
            past_args = (cache_k.reshape(batch, WINDOW, KV_A).astype(BF16),
                         cache_v.reshape(batch, WINDOW, KV_A).astype(BF16), state0.astype(F32))
        oa, orr, state = _mixers(z, sinks, batch, seq_len, past_args)

    weights = (w_branch_attn[order, :].astype(BF16), w_branch_ret.astype(BF16), w_out.astype(BF16),
               g_mix_post.reshape(1, d_model), g_ffn_pre.reshape(1, d_model),
               w_ffn_gate.astype(BF16), w_ffn_up.astype(BF16), w_ffn_down.astype(BF16),
               g_ffn_post.reshape(1, d_model), w_ple_proj.astype(BF16), w_ple_gate.astype(BF16))
    y = _output_stage(oa, orr, gates, x2, p2, weights).reshape(batch, seq_len, d_model)

    tail = kv32.shape[0] // batch
    kv32 = kv32.reshape(batch, tail, 2, ATTN_KV_HEADS, ATTN_HEAD_DIM)
    k_new, v_new = kv32[:, :, 0], kv32[:, :, 1]
    if past is not None:
        n_win = past[0].shape[1]
        k_new = jnp.concatenate([past[0], k_new], axis=1)[:, -n_win:]
        v_new = jnp.concatenate([past[1], v_new], axis=1)[:, -n_win:]
    return y, k_new, v_new, state


def kernel(x_prompt, x_sample, p_prompt, p_sample, cache_attn_k, cache_attn_v, state_ret, g_mix_pre, w_in, attn_sinks, w_branch_attn, w_branch_ret, w_out, g_mix_post, g_ffn_pre, w_ffn_gate, w_ffn_up, w_ffn_down, g_ffn_post, w_ple_proj, w_ple_gate):
    depth = w_in.shape[0]
    assert cache_attn_k.shape[2] == WINDOW, "the rolling window must be full"
    pos_prompt = jnp.arange(x_prompt.shape[1])
    pos_sample = PAST_LEN + jnp.arange(x_sample.shape[1])
    y_p, y_s = x_prompt, x_sample
    outs = [[] for _ in range(6)]
    for i in range(depth):
        w_i = (g_mix_pre[i], w_in[i], attn_sinks[i], w_branch_attn[i], w_branch_ret[i], w_out[i],
               g_mix_post[i], g_ffn_pre[i], w_ffn_gate[i], w_ffn_up[i], w_ffn_down[i], g_ffn_post[i],
               w_ple_proj[i], w_ple_gate[i])
        y_p, kp, vp, rp = _layer(y_p, p_prompt[i], pos_prompt, None, w_i)
        y_s, ks, vs, rs = _layer(y_s, p_sample[i], pos_sample,
                                 (cache_attn_k[i], cache_attn_v[i], state_ret[i]), w_i)
        for lst, val in zip(outs, (kp, vp, rp.astype(x_prompt.dtype), ks, vs, rs.astype(x_sample.dtype))):
            lst.append(val)
    return (y_p, y_s) + tuple(jnp.stack(lst) for lst in outs)
```

```python
import functools

import jax
import jax.numpy as jnp
import numpy as np
from jax import lax
from jax.experimental import pallas as pl
from jax.experimental.pallas import tpu as pltpu

CHUNK = 64
WINDOW = 128
ATTN_HEADS = 8
ATTN_KV_HEADS = 2
ATTN_GROUP = ATTN_HEADS // ATTN_KV_HEADS
ATTN_HEAD_DIM = 64
RET_HEADS = 4
RET_KEY_DIM = 128
RET_VALUE_DIM = 256
RET_ROPE_BASE = 10000.0
NORM_EPS = 1e-6
GN_EPS = 1e-5
PAST_LEN = 2048

Q_A = ATTN_HEADS * ATTN_HEAD_DIM
KV_A = ATTN_KV_HEADS * ATTN_HEAD_DIM
QK_R = RET_HEADS * RET_KEY_DIM
V_R = RET_HEADS * RET_VALUE_DIM

QA_OFF = 0
KA_OFF = QA_OFF + Q_A
VA_OFF = KA_OFF + KV_A
QR_OFF = VA_OFF + KV_A
KR_OFF = QR_OFF + QK_R
VR_OFF = KR_OFF + QK_R
GR_OFF = VR_OFF + V_R
Z_WIDTH = GR_OFF + V_R

VMEM_LIMIT_BYTES = 56 * 1024 * 1024
TOKEN_TILE = 512
MIXER_TILE = 256
RET_BLOCK = 256
ATT_GROUP = 128
FFN_CHUNK = 512

BF16 = jnp.bfloat16
F32 = jnp.float32
LOG2E = float(np.log2(np.e))


def _dot(a, b):
    return jnp.dot(a, b, preferred_element_type=F32)


def _dot_nt(a, b):
    return lax.dot_general(a, b, (((1,), (1,)), ((), ())), preferred_element_type=F32)


def _dot_tn(a, b):
    return lax.dot_general(a, b, (((0,), (0,)), ((), ())), preferred_element_type=F32)


def _sigmoid(x):
    return 1.0 / (1.0 + jnp.exp(-x))


def _rmsnorm(x, g):
    return x * lax.rsqrt(jnp.mean(x * x, axis=-1, keepdims=True) + NORM_EPS) * g


def _resident(shape):
    return pl.BlockSpec(shape, lambda *_: (0,) * len(shape), pipeline_mode=pl.Buffered(1))


def _project_qk(x_ref, g_ref, wq_ref, w_ref, cos_ref, sin_ref, qdec_ref, kdec_ref, z_ref):
    h = _rmsnorm(x_ref[...], g_ref[...]).astype(BF16)

    z_ref[:, QA_OFF:QA_OFF + Q_A] = (_dot(h, wq_ref[...]) * (ATTN_HEAD_DIM ** -0.5 * LOG2E)).astype(BF16)
    kv = _dot(h, w_ref[:, KA_OFF:KA_OFF + 2 * KV_A])
    z_ref[:, KA_OFF:KA_OFF + 2 * KV_A] = kv.astype(BF16)

    cos = cos_ref[...]
    sin = sin_ref[...]
    for off, dec_ref in ((QR_OFF, qdec_ref), (KR_OFF, kdec_ref)):
        z = _dot(h, w_ref[:, off:off + QK_R])
        for hh in range(RET_HEADS):
            zh = z[:, hh * RET_KEY_DIM:(hh + 1) * RET_KEY_DIM]
            rot = (zh * cos + pltpu.roll(zh, RET_KEY_DIM // 2, 1) * sin) * dec_ref[hh]
            z_ref[:, off + hh * RET_KEY_DIM:off + (hh + 1) * RET_KEY_DIM] = rot.astype(BF16)
    return h, kv


def _project_rest(h, w_ref, z_ref, gates_ref):
    z_ref[:, VR_OFF:VR_OFF + V_R] = _dot(h, w_ref[:, VR_OFF:VR_OFF + V_R]).astype(BF16)
    g = _dot(h, w_ref[:, GR_OFF:GR_OFF + V_R])
    z_ref[:, GR_OFF:GR_OFF + V_R] = (g * _sigmoid(g)).astype(BF16)
    d_model = gates_ref.shape[1] // 2
    for j in range(2):
        gates_ref[:, j * d_model:(j + 1) * d_model] = _dot(
            h, w_ref[:, Z_WIDTH + j * d_model:Z_WIDTH + (j + 1) * d_model]).astype(BF16)


def _attend(z_ref, sink_ref, bias_ref, oa_ref, kbuf, vbuf, *, tile, att_group, use_bias, seq_start):
    kbuf[WINDOW:, :] = z_ref[:, KA_OFF:KA_OFF + KV_A]
    vbuf[WINDOW:, 0:KV_A] = z_ref[:, VA_OFF:VA_OFF + KV_A]

    low_half = lax.broadcasted_iota(jnp.int32, (att_group, 2 * ATTN_HEAD_DIM), 1) < ATTN_HEAD_DIM
    win = WINDOW + att_group
    for gi in range(tile // att_group):
        rows = slice(gi * att_group, (gi + 1) * att_group)
        q_tiles = [z_ref[rows, QA_OFF + j * 128:QA_OFF + (j + 1) * 128] for j in range(ATTN_GROUP)]
        zero = jnp.zeros_like(q_tiles[0])
        lhs = jnp.concatenate([jnp.where(low_half, q, zero) for q in q_tiles]
                              + [jnp.where(low_half, zero, q) for q in q_tiles], axis=0)
        k_win = kbuf[gi * att_group:gi * att_group + win, :]
        v_win = vbuf[gi * att_group:gi * att_group + win, :]
        s_all = _dot_nt(lhs, k_win)
        if use_bias:
            bias = bias_ref[0] if (seq_start is None or gi > 0) else bias_ref[jnp.where(seq_start, 1, 0)]
        probs, sink_terms = [], []
        for r in range(ATTN_HEADS):
            s = s_all[r * att_group:(r + 1) * att_group]
            if use_bias:
                s = s + bias
            sink = sink_ref[r] * LOG2E
            m = jnp.maximum(jnp.max(s, axis=1, keepdims=True), sink)
            sink_terms.append(jnp.exp2(sink - m))
            probs.append(jnp.exp2(s - m).astype(BF16))
        o = _dot(jnp.concatenate(probs, axis=0), v_win)

        def head_out(r):
            blk = o[r * att_group:(r + 1) * att_group]
            return blk[:, 0:KV_A] * (1.0 / (blk[:, KV_A:] + sink_terms[r]))

        for j in range(ATTN_GROUP):
            oa_ref[rows, j * 128:(j + 1) * 128] = jnp.where(
                low_half, head_out(j), head_out(ATTN_GROUP + j)).astype(BF16)

    kbuf[0:WINDOW, :] = kbuf[tile:tile + WINDOW, :]
    vbuf[0:WINDOW, 0:KV_A] = vbuf[tile:tile + WINDOW, 0:KV_A]


def _retain(z_ref, tril_ref, or_ref, state, *, tile, ret_block, state_decay):
    for bi in range(tile // ret_block):
        rows = slice(bi * ret_block, (bi + 1) * ret_block)
        for hh in range(RET_HEADS):
            vcols = slice(hh * RET_VALUE_DIM, (hh + 1) * RET_VALUE_DIM)
            q = z_ref[rows, QR_OFF + hh * RET_KEY_DIM:QR_OFF + (hh + 1) * RET_KEY_DIM]
            k = z_ref[rows, KR_OFF + hh * RET_KEY_DIM:KR_OFF + (hh + 1) * RET_KEY_DIM]
            v = z_ref[rows, VR_OFF + hh * RET_VALUE_DIM:VR_OFF + (hh + 1) * RET_VALUE_DIM]
            st = state[hh]
            o_r = _dot((_dot_nt(q, k) * tril_ref[...]).astype(BF16), v) + _dot(q, st.astype(BF16))
            state[hh] = state_decay[hh] * (st + _dot_tn(k, v))
            mu = jnp.mean(o_r, axis=1, keepdims=True)
            cen = o_r - mu
            var = jnp.mean(cen * cen, axis=1, keepdims=True)
            gate = z_ref[rows, GR_OFF + hh * RET_VALUE_DIM:GR_OFF + (hh + 1) * RET_VALUE_DIM].astype(F32)
            or_ref[rows, vcols] = (cen * lax.rsqrt(var + GN_EPS) * gate).astype(BF16)


def _init_window(kbuf, vbuf, k_rows, v_rows):
    kbuf[0:WINDOW, :] = k_rows
    vbuf[0:WINDOW, 0:KV_A] = v_rows
    vbuf[:, KV_A:] = jnp.ones((vbuf.shape[0], KV_A), BF16)


def _mix_config(tile):
    ret_block = min(RET_BLOCK, tile)
    att_group = min(ATT_GROUP, tile)
    assert tile % ret_block == 0 and tile % att_group == 0 and att_group % CHUNK == 0
    tril = jnp.asarray(np.tril(np.ones((ret_block, ret_block))), dtype=F32)
    tables = (_attention_bias(att_group), tril)
    att_cfg = dict(tile=tile, att_group=att_group)
    ret_cfg = dict(tile=tile, ret_block=ret_block, state_decay=_state_decay(ret_block))
    return att_cfg, ret_cfg, tables


STATE_SHAPE = (RET_HEADS, RET_KEY_DIM, RET_VALUE_DIM)


def _front_kernel(x_ref, g_ref, wq_ref, w_ref, cos_ref, sin_ref, qdec_ref, kdec_ref, sink_ref, bias_ref, tril_ref,
                  oa_ref, or_ref, gates_ref, kv32_ref, st_out_ref,
                  z_even, z_odd, kbuf, vbuf, state, *, tiles_per_seq, tail_rows, att_cfg, ret_cfg):
    i = pl.program_id(0)
    tile = att_cfg["tile"]
    no_rows = jnp.zeros((WINDOW, KV_A), BF16)

    @pl.when(i == 0)
    def _():
        z_odd[...] = jnp.zeros(z_odd.shape, BF16)
        _init_window(kbuf, vbuf, no_rows, no_rows)
        state[...] = jnp.zeros(state.shape, F32)

    def step(z_write, z_read):
        seq_start = (jnp.maximum(i - 1, 0) % tiles_per_seq) == 0
        kbuf[0:WINDOW, :] = jnp.where(seq_start, no_rows, kbuf[0:WINDOW, :])
        vbuf[0:WINDOW, 0:KV_A] = jnp.where(seq_start, no_rows, vbuf[0:WINDOW, 0:KV_A])
        state[...] = jnp.where(seq_start, jnp.zeros(state.shape, F32), state[...])
        _retain(z_read, tril_ref, or_ref, state, **ret_cfg)
        st_out_ref[0] = state[...]
        h, kv = _project_qk(x_ref, g_ref, wq_ref, w_ref, cos_ref, sin_ref, qdec_ref, kdec_ref, z_write)
        kv32_ref[...] = kv[tile - tail_rows:, :]
        _attend(z_read, sink_ref, bias_ref, oa_ref, kbuf, vbuf, use_bias=True, seq_start=seq_start, **att_cfg)
        _project_rest(h, w_ref, z_write, gates_ref)

    pl.when(i % 2 == 0)(lambda: step(z_even, z_odd))
    pl.when(i % 2 == 1)(lambda: step(z_odd, z_even))


def _front_fused(x2, g_pre, wq_bf16, w_in_bf16, cos_tab, sin_tab, sinks, batch, seq_len):
    n, d_model = x2.shape
    d_in = w_in_bf16.shape[1]
    tile = TOKEN_TILE
    assert seq_len % tile == 0 and tile >= WINDOW
    tiles_per_seq = seq_len // tile
    n_tiles = n // tile
    tail_rows = WINDOW
    att_cfg, ret_cfg, tables = _mix_config(tile)
    dec_tabs = _projection_decay(tile, ret_cfg["ret_block"])

    def proj_tile(i):
        return jnp.minimum(i, n_tiles - 1)

    def mix_tile(i):
        return jnp.maximum(i - 1, 0)

    def mixed_rows(width):
        return pl.BlockSpec((tile, width), lambda i: (mix_tile(i), 0))

    in_specs = [pl.BlockSpec((tile, d_model), lambda i: (proj_tile(i), 0)),
                _resident((1, d_model)), _resident((d_model, Q_A)), _resident((d_model, d_in)),
                pl.BlockSpec((tile, RET_KEY_DIM), lambda i: (proj_tile(i) % tiles_per_seq, 0)),
                pl.BlockSpec((tile, RET_KEY_DIM), lambda i: (proj_tile(i) % tiles_per_seq, 0)),
                _resident(dec_tabs[0].shape), _resident(dec_tabs[1].shape),
                pl.BlockSpec(memory_space=pltpu.SMEM)]
    in_specs += [_resident(tab.shape) for tab in tables]
    out_specs = [mixed_rows(Q_A), mixed_rows(V_R),
                 pl.BlockSpec((tile, 2 * d_model), lambda i: (proj_tile(i), 0)),
                 pl.BlockSpec((tail_rows, 2 * KV_A), lambda i: (proj_tile(i) // tiles_per_seq, 0)),
                 pl.BlockSpec((1,) + STATE_SHAPE, lambda i: (mix_tile(i) // tiles_per_seq, 0, 0, 0))]
    out_shape = [jax.ShapeDtypeStruct((n, Q_A), BF16), jax.ShapeDtypeStruct((n, V_R), BF16),
                 jax.ShapeDtypeStruct((n, 2 * d_model), BF16),
                 jax.ShapeDtypeStruct((batch * tail_rows, 2 * KV_A), F32),
                 jax.ShapeDtypeStruct((batch,) + STATE_SHAPE, F32)]
    return pl.pallas_call(
        functools.partial(_front_kernel, tiles_per_seq=tiles_per_seq, tail_rows=tail_rows,
                          att_cfg=att_cfg, ret_cfg=ret_cfg),
        grid=(n_tiles + 1,),
        in_specs=in_specs,
        out_specs=out_specs,
        out_shape=out_shape,
        scratch_shapes=[pltpu.VMEM((tile, Z_WIDTH), BF16), pltpu.VMEM((tile, Z_WIDTH), BF16),
                        pltpu.VMEM((WINDOW + tile, KV_A), BF16), pltpu.VMEM((WINDOW + tile, 2 * KV_A), BF16),
                        pltpu.VMEM(STATE_SHAPE, F32)],
        compiler_params=pltpu.CompilerParams(
            dimension_semantics=("arbitrary",), vmem_limit_bytes=VMEM_LIMIT_BYTES),
        name="front",
    )(x2, g_pre, wq_bf16, w_in_bf16, cos_tab, sin_tab, *dec_tabs, sinks, *tables)


def _inproj_kernel(x_ref, g_ref, wq_ref, w_ref, cos_ref, sin_ref, qdec_ref, kdec_ref, z_ref, gates_ref, kv32_ref):
    h, kv = _project_qk(x_ref, g_ref, wq_ref, w_ref, cos_ref, sin_ref, qdec_ref, kdec_ref, z_ref)
    kv32_ref[...] = kv
    _project_rest(h, w_ref, z_ref, gates_ref)


def _in_projection(x2, g_pre, wq_bf16, w_in_bf16, cos_tab, sin_tab, seq_len, ret_block):
    n, d_model = x2.shape
    d_in = w_in_bf16.shape[1]
    tile = min(TOKEN_TILE, n)
    assert n % tile == 0 and tile % seq_len == 0 and seq_len <= WINDOW and seq_len % ret_block == 0
    cos_tab = np.tile(cos_tab, (tile // seq_len, 1))
    sin_tab = np.tile(sin_tab, (tile // seq_len, 1))
    dec_tabs = _projection_decay(tile, ret_block)

    def row_spec(width):
        return pl.BlockSpec((tile, width), lambda i: (i, 0))

    tab_spec = pl.BlockSpec((tile, RET_KEY_DIM), lambda i: (0, 0))
    out_widths = (Z_WIDTH, 2 * d_model)
    out_shape = [jax.ShapeDtypeStruct((n, w), BF16) for w in out_widths]
    out_shape.append(jax.ShapeDtypeStruct((n, 2 * KV_A), F32))
    return pl.pallas_call(
        _inproj_kernel,
        grid=(n // tile,),
        in_specs=[row_spec(d_model), _resident((1, d_model)), _resident((d_model, Q_A)),
                  _resident((d_model, d_in)), tab_spec, tab_spec,
                  _resident(dec_tabs[0].shape), _resident(dec_tabs[1].shape)],
        out_specs=[row_spec(w) for w in out_widths] + [row_spec(2 * KV_A)],
        out_shape=out_shape,
        compiler_params=pltpu.CompilerParams(
            dimension_semantics=("arbitrary",), vmem_limit_bytes=VMEM_LIMIT_BYTES),
        name="in_projection",
    )(x2, g_pre, wq_bf16, w_in_bf16, cos_tab, sin_tab, *dec_tabs)


def _mixer_kernel(*refs, has_past, use_bias, att_cfg, ret_cfg):
    z_ref, sink_ref, bias_ref, tril_ref = refs[:4]
    rest = refs[4:]
    if has_past:
        ck_ref, cv_ref, st0_ref = rest[:3]
        rest = rest[3:]
    oa_ref, or_ref, st_out_ref, kbuf, vbuf, state = rest
    t = pl.program_id(1)

    @pl.when(t == 0)
    def _():
        if has_past:
            _init_window(kbuf, vbuf, ck_ref[0], cv_ref[0])
            state[...] = st0_ref[0]
        else:
            _init_window(kbuf, vbuf, jnp.zeros((WINDOW, KV_A), BF16), jnp.zeros((WINDOW, KV_A), BF16))
            state[...] = jnp.zeros(state.shape, F32)

    _attend(z_ref, sink_ref, bias_ref, oa_ref, kbuf, vbuf,
            use_bias=use_bias, seq_start=None if has_past else t == 0, **att_cfg)
    _retain(z_ref, tril_ref, or_ref, state, **ret_cfg)

    @pl.when(t == pl.num_programs(1) - 1)
    def _():
        st_out_ref[0] = state[...]


def _mixers(z, sinks, batch, seq_len, past=None):
    tile = min(MIXER_TILE, seq_len)
    assert seq_len % tile == 0
    nt = seq_len // tile
    n = batch * seq_len
    att_cfg, ret_cfg, tables = _mix_config(tile)
    use_bias = not (past is not None and att_cfg["att_group"] == CHUNK)

    def row_spec(width):
        return pl.BlockSpec((tile, width), lambda b, t: (b * nt + t, 0))

    in_specs = [row_spec(Z_WIDTH), pl.BlockSpec(memory_space=pltpu.SMEM)]
    in_specs += [_resident(tab.shape) for tab in tables]
    args = [z, sinks] + list(tables)
    if past is not None:
        in_specs += [pl.BlockSpec((1, WINDOW, KV_A), lambda b, t: (b, 0, 0)),
                     pl.BlockSpec((1, WINDOW, KV_A), lambda b, t: (b, 0, 0)),
                     pl.BlockSpec((1,) + STATE_SHAPE, lambda b, t: (b, 0, 0, 0))]
        args += list(past)
    return pl.pallas_call(
        functools.partial(_mixer_kernel, has_past=past is not None, use_bias=use_bias,
                          att_cfg=att_cfg, ret_cfg=ret_cfg),
        grid=(batch, nt),
        in_specs=in_specs,
        out_specs=[row_spec(Q_A), row_spec(V_R),
                   pl.BlockSpec((1,) + STATE_SHAPE, lambda b, t: (b, 0, 0, 0))],
        out_shape=[jax.ShapeDtypeStruct((n, Q_A), BF16), jax.ShapeDtypeStruct((n, V_R), BF16),
                   jax.ShapeDtypeStruct((batch,) + STATE_SHAPE, F32)],
        scratch_shapes=[pltpu.VMEM((WINDOW + tile, KV_A), BF16), pltpu.VMEM((WINDOW + tile, 2 * KV_A), BF16),
                        pltpu.VMEM(STATE_SHAPE, F32)],
        compiler_params=pltpu.CompilerParams(
            dimension_semantics=("arbitrary", "arbitrary"), vmem_limit_bytes=VMEM_LIMIT_BYTES),
        name="mixers",
    )(*args)


def _post_kernel(oa_ref, or_ref, gates_ref, x_ref, p_ref,
                 wba_ref, wbr_ref, wout_ref, gpost_ref, gfpre_ref, wg_ref, wu_ref, wd_ref, gfpost_ref,
                 wpp_ref, wpg_ref, y_ref, *, ffn_slabs):
    d_model = x_ref.shape[1]
    gate_a = gates_ref[:, :d_model].astype(F32)
    gate_r = gates_ref[:, d_model:].astype(F32)
    merged = (_sigmoid(gate_a) * _dot(oa_ref[...], wba_ref[...])
              + _sigmoid(gate_r) * _dot(or_ref[...], wbr_ref[...]))
    y = x_ref[...] + _rmsnorm(_dot(merged.astype(BF16), wout_ref[...]), gpost_ref[...])

    h = _rmsnorm(y, gfpre_ref[...]).astype(BF16)
    f = None
    for lo, hi in ffn_slabs:
        gate = _dot(h, wg_ref[:, lo:hi])
        act = (gate * _sigmoid(gate) * _dot(h, wu_ref[:, lo:hi])).astype(BF16)
        part = _dot(act, wd_ref[lo:hi, :])
        f = part if f is None else f + part
    y = y + _rmsnorm(f, gfpost_ref[...])

    emb = _dot(p_ref[...].astype(BF16), wpp_ref[...])
    y_ref[...] = y + emb * _sigmoid(_dot(y.astype(BF16), wpg_ref[...]))


def _output_stage(oa, orr, gates, x2, p2, weights):
    n, d_model = x2.shape
    tile = min(TOKEN_TILE, n)
    assert n % tile == 0
    ffn_hidden = weights[7].shape[0]
    ffn_slabs = tuple((lo, min(lo + FFN_CHUNK, ffn_hidden)) for lo in range(0, ffn_hidden, FFN_CHUNK))

    def row_spec(width):
        return pl.BlockSpec((tile, width), lambda i: (i, 0))

    acts = (oa, orr, gates, x2, p2)
    return pl.pallas_call(
        functools.partial(_post_kernel, ffn_slabs=ffn_slabs),
        grid=(n // tile,),
        in_specs=[row_spec(a.shape[1]) for a in acts] + [_resident(w.shape) for w in weights],
        out_specs=row_spec(d_model),
        out_shape=jax.ShapeDtypeStruct((n, d_model), F32),
        compiler_params=pltpu.CompilerParams(
            dimension_semantics=("arbitrary",), vmem_limit_bytes=VMEM_LIMIT_BYTES),
        name="output_stage",
    )(*acts, *weights)


def _rotary_tables(pos):
    half = RET_KEY_DIM // 2
    inv = 1.0 / (RET_ROPE_BASE ** np.linspace(0.0, 1.0, half))
    ang = np.asarray(pos, np.float64)[:, None] * inv[None, :]
    cos, sin = np.cos(ang), np.sin(ang)
    return (np.concatenate([cos, cos], axis=1).astype(np.float32),
            np.concatenate([-sin, sin], axis=1).astype(np.float32))


def _log_decay():
    return np.log1p(-np.exp2(-5.0 - np.arange(RET_HEADS, dtype=np.float64)))


def _projection_decay(tile, block):
    pos = (np.arange(tile) % block) + 1.0
    q_dec = np.exp(_log_decay()[:, None] * pos[None, :])
    k_dec = np.exp(-_log_decay()[:, None] * pos[None, :]) * (RET_KEY_DIM ** -0.5)
    shape = (RET_HEADS, tile, RET_KEY_DIM)
    return tuple(np.ascontiguousarray(np.broadcast_to(t[:, :, None], shape), dtype=np.float32)
                 for t in (q_dec, k_dec))


def _state_decay(block):
    return tuple(float(v) for v in np.exp(_log_decay() * block))


def _attention_bias(group):
    q_chunk = np.arange(group)[:, None] // CHUNK
    k_chunk = np.arange(WINDOW + group)[None, :] // CHUNK
    visible = (k_chunk >= q_chunk) & (k_chunk <= q_chunk + WINDOW // CHUNK)
    at_start = visible & (k_chunk >= WINDOW // CHUNK)
    return jnp.asarray(np.where(np.stack([visible, at_start]), 0.0, -np.inf), dtype=F32)


def _head_pair_order():
    order = []
    for j in range(ATTN_GROUP):
        for kvh in range(ATTN_KV_HEADS):
            head = kvh * ATTN_GROUP + j
            order.extend(range(head * ATTN_HEAD_DIM, (head + 1) * ATTN_HEAD_DIM))
    return np.asarray(order, dtype=np.int32)


def _layer(x, p, pos, past, w):
    (g_mix_pre, w_in, attn_sinks, w_branch_attn, w_branch_ret, w_out, g_mix_post, g_ffn_pre,
     w_ffn_gate, w_ffn_up, w_ffn_down, g_ffn_post, w_ple_proj, w_ple_gate) = w
    batch, seq_len, d_model = x.shape
    n = batch * seq_len
    x2 = x.reshape(n, d_model)
    p2 = p.reshape(n, p.shape[-1])

    order = _head_pair_order()
    w_in_b = w_in.astype(BF16)
    wq_b = w_in[:, :Q_A][:, order].astype(BF16)
    cos_tab, sin_tab = _rotary_tables(pos)
    g_pre = g_mix_pre.reshape(1, d_model)
    sinks = attn_sinks.astype(F32)
    if past is None and seq_len >= TOKEN_TILE:
        oa, orr, gates, kv32, state = _front_fused(
            x2, g_pre, wq_b, w_in_b, cos_tab, sin_tab, sinks, batch, seq_len)
    else:
        ret_block = min(RET_BLOCK, MIXER_TILE, seq_len)
        z, gates, kv32 = _in_projection(x2, g_pre, wq_b, w_in_b, cos_tab, sin_tab, seq_len, ret_block)
        past_args = None
        if past is not None:
            cache_k, cache_v, state0 = past
            past_args = (cache_k.reshape(batch, WINDOW, KV_A).astype(BF16),
                         cache_v.reshape(batch, WINDOW, KV_A).astype(BF16), state0.astype(F32))
        oa, orr, state = _mixers(z, sinks, batch, seq_len, past_args)

    weights = (w_branch_attn[order, :].astype(BF16), w_branch_ret.astype(BF16), w_out.astype(BF16),
               g_mix_post.reshape(1, d_model), g_ffn_pre.reshape(1, d_model),
               w_ffn_gate.astype(BF16), w_ffn_up.astype(BF16), w_ffn_down.astype(BF16),
               g_ffn_post.reshape(1, d_model), w_ple_proj.astype(BF16), w_ple_gate.astype(BF16))
    y = _output_stage(oa, orr, gates, x2, p2, weights).reshape(batch, seq_len, d_model)

    tail = kv32.shape[0] // batch
    kv32 = kv32.reshape(batch, tail, 2, ATTN_KV_HEADS, ATTN_HEAD_DIM)
    k_new, v_new = kv32[:, :, 0], kv32[:, :, 1]
    if past is not None:
        n_win = past[0].shape[1]
        k_new = jnp.concatenate([past[0], k_new], axis=1)[:, -n_win:]
        v_new = jnp.concatenate([past[1], v_new], axis=1)[:, -n_win:]
    return y, k_new, v_new, state


def kernel(x_prompt, x_sample, p_prompt, p_sample, cache_attn_k, cache_attn_v, state_ret, g_mix_pre, w_in, attn_sinks, w_branch_attn, w_branch_ret, w_out, g_mix_post, g_ffn_pre, w_ffn_gate, w_ffn_up, w_ffn_down, g_ffn_post, w_ple_proj, w_ple_gate):
    depth = w_in.shape[0]
    assert cache_attn_k.shape[2] == WINDOW, "the rolling window must be full"
    pos_prompt = np.arange(x_prompt.shape[1])
    pos_sample = PAST_LEN + np.arange(x_sample.shape[1])
    y_p, y_s = x_prompt, x_sample
    outs = [[] for _ in range(6)]
    for i in range(depth):
        w_i = (g_mix_pre[i], w_in[i], attn_sinks[i], w_branch_attn[i], w_branch_ret[i], w_out[i],
               g_mix_post[i], g_ffn_pre[i], w_ffn_gate[i], w_ffn_up[i], w_ffn_down[i], g_ffn_post[i],
               w_ple_proj[i], w_ple_gate[i])
        y_p, kp, vp, rp = _layer(y_p, p_prompt[i], pos_prompt, None, w_i)
        y_s, ks, vs, rs = _layer(y_s, p_sample[i], pos_sample,
                                 (cache_attn_k[i], cache_attn_v[i], state_ret[i]), w_i)
        for lst, val in zip(outs, (kp, vp, rp.astype(x_prompt.dtype), ks, vs, rs.astype(x_sample.dtype))):
            lst.append(val)
    return (y_p, y_s) + tuple(jnp.stack(lst) for lst in outs)
```

```python
import functools

import jax
import jax.numpy as jnp
import numpy as np
from jax import lax
from jax.experimental import pallas as pl
from jax.experimental.pallas import tpu as pltpu

CHUNK = 64
WINDOW = 128
ATTN_HEADS = 8
ATTN_KV_HEADS = 2
ATTN_GROUP = ATTN_HEADS // ATTN_KV_HEADS
ATTN_HEAD_DIM = 64
RET_HEADS = 4
RET_KEY_DIM = 128
RET_VALUE_DIM = 256
RET_ROPE_BASE = 10000.0
NORM_EPS = 1e-6
GN_EPS = 1e-5
PAST_LEN = 2048

Q_A = ATTN_HEADS * ATTN_HEAD_DIM
KV_A = ATTN_KV_HEADS * ATTN_HEAD_DIM
QK_R = RET_HEADS * RET_KEY_DIM
V_R = RET_HEADS * RET_VALUE_DIM

QA_OFF = 0
KA_OFF = QA_OFF + Q_A
VA_OFF = KA_OFF + KV_A
QR_OFF = VA_OFF + KV_A
KR_OFF = QR_OFF + QK_R
VR_OFF = KR_OFF + QK_R
GR_OFF = VR_OFF + V_R
Z_WIDTH = GR_OFF + V_R

VMEM_LIMIT_BYTES = 56 * 1024 * 1024
TOKEN_TILE = 512
MIXER_TILE = 256
RET_BLOCK = 256
ATT_GROUP = 128
FFN_CHUNK = 512

BF16 = jnp.bfloat16
F32 = jnp.float32
LOG2E = float(np.log2(np.e))


def _dot(a, b):
    return jnp.dot(a, b, preferred_element_type=F32)


def _dot_nt(a, b):
    return lax.dot_general(a, b, (((1,), (1,)), ((), ())), preferred_element_type=F32)


def _dot_tn(a, b):
    return lax.dot_general(a, b, (((0,), (0,)), ((), ())), preferred_element_type=F32)


def _sigmoid(x):
    return 0.5 * jnp.tanh(0.5 * x) + 0.5


def _rmsnorm(x, g):
    return x * lax.rsqrt(jnp.mean(x * x, axis=-1, keepdims=True) + NORM_EPS) * g


def _resident(shape):
    return pl.BlockSpec(shape, lambda *_: (0,) * len(shape), pipeline_mode=pl.Buffered(1))


def _project_qk(x_ref, g_ref, wq_ref, w_ref, cos_ref, sin_ref, qdec_ref, kdec_ref, z_ref):
    h = _rmsnorm(x_ref[...], g_ref[...]).astype(BF16)

    z_ref[:, QA_OFF:QA_OFF + Q_A] = (_dot(h, wq_ref[...]) * (ATTN_HEAD_DIM ** -0.5 * LOG2E)).astype(BF16)
    kv = _dot(h, w_ref[:, KA_OFF:KA_OFF + 2 * KV_A])
    z_ref[:, KA_OFF:KA_OFF + 2 * KV_A] = kv.astype(BF16)

    cos = cos_ref[...]
    sin = sin_ref[...]
    for off, dec_ref in ((QR_OFF, qdec_ref), (KR_OFF, kdec_ref)):
        z = _dot(h, w_ref[:, off:off + QK_R])
        for hh in range(RET_HEADS):
            zh = z[:, hh * RET_KEY_DIM:(hh + 1) * RET_KEY_DIM]
            rot = (zh * cos + pltpu.roll(zh, RET_KEY_DIM // 2, 1) * sin) * dec_ref[hh]
            z_ref[:, off + hh * RET_KEY_DIM:off + (hh + 1) * RET_KEY_DIM] = rot.astype(BF16)
    return h, kv


def _project_rest(h, w_ref, z_ref, gates_ref):
    z_ref[:, VR_OFF:VR_OFF + V_R] = _dot(h, w_ref[:, VR_OFF:VR_OFF + V_R]).astype(BF16)
    g = _dot(h, w_ref[:, GR_OFF:GR_OFF + V_R])
    z_ref[:, GR_OFF:GR_OFF + V_R] = (g * _sigmoid(g)).astype(BF16)
    d_model = gates_ref.shape[1] // 2
    for j in range(2):
        gates_ref[:, j * d_model:(j + 1) * d_model] = _dot(
            h, w_ref[:, Z_WIDTH + j * d_model:Z_WIDTH + (j + 1) * d_model]).astype(BF16)


def _attend(z_ref, sink_ref, bias_ref, oa_ref, kbuf, vbuf, *, tile, att_group, use_bias, seq_start):
    kbuf[WINDOW:, :] = z_ref[:, KA_OFF:KA_OFF + KV_A]
    vbuf[WINDOW:, 0:KV_A] = z_ref[:, VA_OFF:VA_OFF + KV_A]

    low_half = lax.broadcasted_iota(jnp.int32, (att_group, 2 * ATTN_HEAD_DIM), 1) < ATTN_HEAD_DIM
    win = WINDOW + att_group
    for gi in range(tile // att_group):
        rows = slice(gi * att_group, (gi + 1) * att_group)
        q_tiles = [z_ref[rows, QA_OFF + j * 128:QA_OFF + (j + 1) * 128] for j in range(ATTN_GROUP)]
        zero = jnp.zeros_like(q_tiles[0])
        lhs = jnp.concatenate([jnp.where(low_half, q, zero) for q in q_tiles]
                              + [jnp.where(low_half, zero, q) for q in q_tiles], axis=0)
        k_win = kbuf[gi * att_group:gi * att_group + win, :]
        v_win = vbuf[gi * att_group:gi * att_group + win, :]
        s_all = _dot_nt(lhs, k_win)
        if use_bias:
            bias = bias_ref[0] if (seq_start is None or gi > 0) else bias_ref[jnp.where(seq_start, 1, 0)]
        probs, sink_terms = [], []
        for r in range(ATTN_HEADS):
            s = s_all[r * att_group:(r + 1) * att_group]
            if use_bias:
                s = s + bias
            sink = sink_ref[r] * LOG2E
            m = jnp.maximum(jnp.max(s, axis=1, keepdims=True), sink)
            sink_terms.append(jnp.exp2(sink - m))
            probs.append(jnp.exp2(s - m).astype(BF16))
        o = _dot(jnp.concatenate(probs, axis=0), v_win)

        def head_out(r):
            blk = o[r * att_group:(r + 1) * att_group]
            return blk[:, 0:KV_A] * (1.0 / (blk[:, KV_A:] + sink_terms[r]))

        for j in range(ATTN_GROUP):
            oa_ref[rows, j * 128:(j + 1) * 128] = jnp.where(
                low_half, head_out(j), head_out(ATTN_GROUP + j)).astype(BF16)

    kbuf[0:WINDOW, :] = kbuf[tile:tile + WINDOW, :]
    vbuf[0:WINDOW, 0:KV_A] = vbuf[tile:tile + WINDOW, 0:KV_A]


def _retain(z_ref, tril_ref, or_ref, state, *, tile, ret_block, state_decay):
    for bi in range(tile // ret_block):
        rows = slice(bi * ret_block, (bi + 1) * ret_block)
        for hh in range(RET_HEADS):
            vcols = slice(hh * RET_VALUE_DIM, (hh + 1) * RET_VALUE_DIM)
            q = z_ref[rows, QR_OFF + hh * RET_KEY_DIM:QR_OFF + (hh + 1) * RET_KEY_DIM]
            k = z_ref[rows, KR_OFF + hh * RET_KEY_DIM:KR_OFF + (hh + 1) * RET_KEY_DIM]
            v = z_ref[rows, VR_OFF + hh * RET_VALUE_DIM:VR_OFF + (hh + 1) * RET_VALUE_DIM]
            st = state[hh]
            o_r = _dot((_dot_nt(q, k) * tril_ref[...]).astype(BF16), v) + _dot(q, st.astype(BF16))
            state[hh] = state_decay[hh] * (st + _dot_tn(k, v))
            mu = jnp.mean(o_r, axis=1, keepdims=True)
            cen = o_r - mu
            var = jnp.mean(cen * cen, axis=1, keepdims=True)
            gate = z_ref[rows, GR_OFF + hh * RET_VALUE_DIM:GR_OFF + (hh + 1) * RET_VALUE_DIM].astype(F32)
            or_ref[rows, vcols] = (cen * lax.rsqrt(var + GN_EPS) * gate).astype(BF16)


def _init_window(kbuf, vbuf, k_rows, v_rows):
    kbuf[0:WINDOW, :] = k_rows
    vbuf[0:WINDOW, 0:KV_A] = v_rows
    vbuf[:, KV_A:] = jnp.ones((vbuf.shape[0], KV_A), BF16)


def _mix_config(tile):
    ret_block = min(RET_BLOCK, tile)
    att_group = min(ATT_GROUP, tile)
    assert tile % ret_block == 0 and tile % att_group == 0 and att_group % CHUNK == 0
    tril = jnp.asarray(np.tril(np.ones((ret_block, ret_block))), dtype=F32)
    tables = (_attention_bias(att_group), tril)
    att_cfg = dict(tile=tile, att_group=att_group)
    ret_cfg = dict(tile=tile, ret_block=ret_block, state_decay=_state_decay(ret_block))
    return att_cfg, ret_cfg, tables


STATE_SHAPE = (RET_HEADS, RET_KEY_DIM, RET_VALUE_DIM)


def _front_kernel(x_ref, g_ref, wq_ref, w_ref, cos_ref, sin_ref, qdec_ref, kdec_ref, sink_ref, bias_ref, tril_ref,
                  oa_ref, or_ref, gates_ref, kv32_ref, st_out_ref,
                  z_even, z_odd, kbuf, vbuf, state, *, tiles_per_seq, tail_rows, att_cfg, ret_cfg):
    i = pl.program_id(0)
    tile = att_cfg["tile"]
    no_rows = jnp.zeros((WINDOW, KV_A), BF16)

    mix_tile_in_seq = jnp.maximum(i - 1, 0) % tiles_per_seq
    seq_start = mix_tile_in_seq == 0

    @pl.when(i == 0)
    def _():
        z_odd[...] = jnp.zeros(z_odd.shape, BF16)
        vbuf[:, KV_A:] = jnp.ones((vbuf.shape[0], KV_A), BF16)

    @pl.when(seq_start)
    def _():
        kbuf[0:WINDOW, :] = no_rows
        vbuf[0:WINDOW, 0:KV_A] = no_rows
        state[...] = jnp.zeros(state.shape, F32)

    def step(z_write, z_read):
        _retain(z_read, tril_ref, or_ref, state, **ret_cfg)
        h, kv = _project_qk(x_ref, g_ref, wq_ref, w_ref, cos_ref, sin_ref, qdec_ref, kdec_ref, z_write)
        kv32_ref[...] = kv[tile - tail_rows:, :]
        _attend(z_read, sink_ref, bias_ref, oa_ref, kbuf, vbuf, use_bias=True, seq_start=seq_start, **att_cfg)
        _project_rest(h, w_ref, z_write, gates_ref)

    pl.when(i % 2 == 0)(lambda: step(z_even, z_odd))
    pl.when(i % 2 == 1)(lambda: step(z_odd, z_even))

    @pl.when(mix_tile_in_seq == tiles_per_seq - 1)
    def _():
        st_out_ref[0] = state[...]


def _front_fused(x2, g_pre, wq_bf16, w_in_bf16, cos_tab, sin_tab, sinks, batch, seq_len):
    n, d_model = x2.shape
    d_in = w_in_bf16.shape[1]
    tile = TOKEN_TILE
    assert seq_len % tile == 0 and tile >= WINDOW
    tiles_per_seq = seq_len // tile
    n_tiles = n // tile
    tail_rows = WINDOW
    att_cfg, ret_cfg, tables = _mix_config(tile)
    dec_tabs = _projection_decay(tile, ret_cfg["ret_block"])

    def proj_tile(i):
        return jnp.minimum(i, n_tiles - 1)

    def mix_tile(i):
        return jnp.maximum(i - 1, 0)

    def mixed_rows(width):
        return pl.BlockSpec((tile, width), lambda i: (mix_tile(i), 0))

    in_specs = [pl.BlockSpec((tile, d_model), lambda i: (proj_tile(i), 0)),
                _resident((1, d_model)), _resident((d_model, Q_A)), _resident((d_model, d_in)),
                pl.BlockSpec((tile, RET_KEY_DIM), lambda i: (proj_tile(i) % tiles_per_seq, 0)),
                pl.BlockSpec((tile, RET_KEY_DIM), lambda i: (proj_tile(i) % tiles_per_seq, 0)),
                _resident(dec_tabs[0].shape), _resident(dec_tabs[1].shape),
                pl.BlockSpec(memory_space=pltpu.SMEM)]
    in_specs += [_resident(tab.shape) for tab in tables]
    out_specs = [mixed_rows(Q_A), mixed_rows(V_R),
                 pl.BlockSpec((tile, 2 * d_model), lambda i: (proj_tile(i), 0)),
                 pl.BlockSpec((tail_rows, 2 * KV_A), lambda i: (proj_tile(i) // tiles_per_seq, 0)),
                 pl.BlockSpec((1,) + STATE_SHAPE, lambda i: (mix_tile(i) // tiles_per_seq, 0, 0, 0))]
    out_shape = [jax.ShapeDtypeStruct((n, Q_A), BF16), jax.ShapeDtypeStruct((n, V_R), BF16),
                 jax.ShapeDtypeStruct((n, 2 * d_model), BF16),
                 jax.ShapeDtypeStruct((batch * tail_rows, 2 * KV_A), F32),
                 jax.ShapeDtypeStruct((batch,) + STATE_SHAPE, F32)]
    return pl.pallas_call(
        functools.partial(_front_kernel, tiles_per_seq=tiles_per_seq, tail_rows=tail_rows,
                          att_cfg=att_cfg, ret_cfg=ret_cfg),
        grid=(n_tiles + 1,),
        in_specs=in_specs,
        out_specs=out_specs,
        out_shape=out_shape,
        scratch_shapes=[pltpu.VMEM((tile, Z_WIDTH), BF16), pltpu.VMEM((tile, Z_WIDTH), BF16),
                        pltpu.VMEM((WINDOW + tile, KV_A), BF16), pltpu.VMEM((WINDOW + tile, 2 * KV_A), BF16),
                        pltpu.VMEM(STATE_SHAPE, F32)],
        compiler_params=pltpu.CompilerParams(
            dimension_semantics=("arbitrary",), vmem_limit_bytes=VMEM_LIMIT_BYTES),
        name="front",
    )(x2, g_pre, wq_bf16, w_in_bf16, cos_tab, sin_tab, *dec_tabs, sinks, *tables)


def _inproj_kernel(x_ref, g_ref, wq_ref, w_ref, cos_ref, sin_ref, qdec_ref, kdec_ref, z_ref, gates_ref, kv32_ref):
    h, kv = _project_qk(x_ref, g_ref, wq_ref, w_ref, cos_ref, sin_ref, qdec_ref, kdec_ref, z_ref)
    kv32_ref[...] = kv
    _project_rest(h, w_ref, z_ref, gates_ref)


def _in_projection(x2, g_pre, wq_bf16, w_in_bf16, cos_tab, sin_tab, seq_len, ret_block):
    n, d_model = x2.shape
    d_in = w_in_bf16.shape[1]
    tile = min(TOKEN_TILE, n)
    assert n % tile == 0 and tile % seq_len == 0 and seq_len <= WINDOW and seq_len % ret_block == 0
    cos_tab = np.tile(cos_tab, (tile // seq_len, 1))
    sin_tab = np.tile(sin_tab, (tile // seq_len, 1))
    dec_tabs = _projection_decay(tile, ret_block)

    def row_spec(width):
        return pl.BlockSpec((tile, width), lambda i: (i, 0))

    tab_spec = pl.BlockSpec((tile, RET_KEY_DIM), lambda i: (0, 0))
    out_widths = (Z_WIDTH, 2 * d_model)
    out_shape = [jax.ShapeDtypeStruct((n, w), BF16) for w in out_widths]
    out_shape.append(jax.ShapeDtypeStruct((n, 2 * KV_A), F32))
    return pl.pallas_call(
        _inproj_kernel,
        grid=(n // tile,),
        in_specs=[row_spec(d_model), _resident((1, d_model)), _resident((d_model, Q_A)),
                  _resident((d_model, d_in)), tab_spec, tab_spec,
                  _resident(dec_tabs[0].shape), _resident(dec_tabs[1].shape)],
        out_specs=[row_spec(w) for w in out_widths] + [row_spec(2 * KV_A)],
        out_shape=out_shape,
        compiler_params=pltpu.CompilerParams(
            dimension_semantics=("arbitrary",), vmem_limit_bytes=VMEM_LIMIT_BYTES),
        name="in_projection",
    )(x2, g_pre, wq_bf16, w_in_bf16, cos_tab, sin_tab, *dec_tabs)


def _mixer_kernel(*refs, has_past, use_bias, att_cfg, ret_cfg):
    z_ref, sink_ref, bias_ref, tril_ref = refs[:4]
    rest = refs[4:]
    if has_past:
        ck_ref, cv_ref, st0_ref = rest[:3]
        rest = rest[3:]
    oa_ref, or_ref, st_out_ref, kbuf, vbuf, state = rest
    t = pl.program_id(1)

    @pl.when(t == 0)
    def _():
        if has_past:
            _init_window(kbuf, vbuf, ck_ref[0], cv_ref[0])
            state[...] = st0_ref[0]
        else:
            _init_window(kbuf, vbuf, jnp.zeros((WINDOW, KV_A), BF16), jnp.zeros((WINDOW, KV_A), BF16))
            state[...] = jnp.zeros(state.shape, F32)

    _attend(z_ref, sink_ref, bias_ref, oa_ref, kbuf, vbuf,
            use_bias=use_bias, seq_start=None if has_past else t == 0, **att_cfg)
    _retain(z_ref, tril_ref, or_ref, state, **ret_cfg)

    @pl.when(t == pl.num_programs(1) - 1)
    def _():
        st_out_ref[0] = state[...]


def _mixers(z, sinks, batch, seq_len, past=None):
    tile = min(MIXER_TILE, seq_len)
    assert seq_len % tile == 0
    nt = seq_len // tile
    n = batch * seq_len
    att_cfg, ret_cfg, tables = _mix_config(tile)
    use_bias = not (past is not None and att_cfg["att_group"] == CHUNK)

    def row_spec(width):
        return pl.BlockSpec((tile, width), lambda b, t: (b * nt + t, 0))

    in_specs = [row_spec(Z_WIDTH), pl.BlockSpec(memory_space=pltpu.SMEM)]
    in_specs += [_resident(tab.shape) for tab in tables]
    args = [z, sinks] + list(tables)
    if past is not None:
        in_specs += [pl.BlockSpec((1, WINDOW, KV_A), lambda b, t: (b, 0, 0)),
                     pl.BlockSpec((1, WINDOW, KV_A), lambda b, t: (b, 0, 0)),
                     pl.BlockSpec((1,) + STATE_SHAPE, lambda b, t: (b, 0, 0, 0))]
        args += list(past)
    return pl.pallas_call(
        functools.partial(_mixer_kernel, has_past=past is not None, use_bias=use_bias,
                          att_cfg=att_cfg, ret_cfg=ret_cfg),
        grid=(batch, nt),
        in_specs=in_specs,
        out_specs=[row_spec(Q_A), row_spec(V_R),
                   pl.BlockSpec((1,) + STATE_SHAPE, lambda b, t: (b, 0, 0, 0))],
        out_shape=[jax.ShapeDtypeStruct((n, Q_A), BF16), jax.ShapeDtypeStruct((n, V_R), BF16),
                   jax.ShapeDtypeStruct((batch,) + STATE_SHAPE, F32)],
        scratch_shapes=[pltpu.VMEM((WINDOW + tile, KV_A), BF16), pltpu.VMEM((WINDOW + tile, 2 * KV_A), BF16),
                        pltpu.VMEM(STATE_SHAPE, F32)],
        compiler_params=pltpu.CompilerParams(
            dimension_semantics=("arbitrary", "arbitrary"), vmem_limit_bytes=VMEM_LIMIT_BYTES),
        name="mixers",
    )(*args)


def _post_kernel(oa_ref, or_ref, gates_ref, x_ref, p_ref,
                 wba_ref, wbr_ref, wout_ref, gpost_ref, gfpre_ref, wg_ref, wu_ref, wd_ref, gfpost_ref,
                 wpp_ref, wpg_ref, y_ref, *, ffn_slabs):
    d_model = x_ref.shape[1]
    gate_a = gates_ref[:, :d_model].astype(F32)
    gate_r = gates_ref[:, d_model:].astype(F32)
    merged = (_sigmoid(gate_a) * _dot(oa_ref[...], wba_ref[...])
              + _sigmoid(gate_r) * _dot(or_ref[...], wbr_ref[...]))
    y = x_ref[...] + _rmsnorm(_dot(merged.astype(BF16), wout_ref[...]), gpost_ref[...])

    h = _rmsnorm(y, gfpre_ref[...]).astype(BF16)
    f = None
    for lo, hi in ffn_slabs:
        gate = _dot(h, wg_ref[:, lo:hi])
        act = (gate * _sigmoid(gate) * _dot(h, wu_ref[:, lo:hi])).astype(BF16)
        part = _dot(act, wd_ref[lo:hi, :])
        f = part if f is None else f + part
    y = y + _rmsnorm(f, gfpost_ref[...])

    emb = _dot(p_ref[...].astype(BF16), wpp_ref[...])
    y_ref[...] = y + emb * _sigmoid(_dot(y.astype(BF16), wpg_ref[...]))


def _output_stage(oa, orr, gates, x2, p2, weights):
    n, d_model = x2.shape
    tile = min(TOKEN_TILE, n)
    assert n % tile == 0
    ffn_hidden = weights[7].shape[0]
    ffn_slabs = tuple((lo, min(lo + FFN_CHUNK, ffn_hidden)) for lo in range(0, ffn_hidden, FFN_CHUNK))

    def row_spec(width):
        return pl.BlockSpec((tile, width), lambda i: (i, 0))

    acts = (oa, orr, gates, x2, p2)
    return pl.pallas_call(
        functools.partial(_post_kernel, ffn_slabs=ffn_slabs),
        grid=(n // tile,),
        in_specs=[row_spec(a.shape[1]) for a in acts] + [_resident(w.shape) for w in weights],
        out_specs=row_spec(d_model),
        out_shape=jax.ShapeDtypeStruct((n, d_model), F32),
        compiler_params=pltpu.CompilerParams(
            dimension_semantics=("arbitrary",), vmem_limit_bytes=VMEM_LIMIT_BYTES),
        name="output_stage",
    )(*acts, *weights)


def _rotary_tables(pos):
    half = RET_KEY_DIM // 2
    inv = 1.0 / (RET_ROPE_BASE ** np.linspace(0.0, 1.0, half))
    ang = np.asarray(pos, np.float64)[:, None] * inv[None, :]
    cos, sin = np.cos(ang), np.sin(ang)
    return (np.concatenate([cos, cos], axis=1).astype(np.float32),
            np.concatenate([-sin, sin], axis=1).astype(np.float32))


def _log_decay():
    return np.log1p(-np.exp2(-5.0 - np.arange(RET_HEADS, dtype=np.float64)))


def _projection_decay(tile, block):
    pos = (np.arange(tile) % block) + 1.0
    q_dec = np.exp(_log_decay()[:, None] * pos[None, :])
    k_dec = np.exp(-_log_decay()[:, None] * pos[None, :]) * (RET_KEY_DIM ** -0.5)
    shape = (RET_HEADS, tile, RET_KEY_DIM)
    return tuple(np.ascontiguousarray(np.broadcast_to(t[:, :, None], shape), dtype=np.float32)
                 for t in (q_dec, k_dec))


def _state_decay(block):
    return tuple(float(v) for v in np.exp(_log_decay() * block))


def _attention_bias(group):
    q_chunk = np.arange(group)[:, None] // CHUNK
    k_chunk = np.arange(WINDOW + group)[None, :] // CHUNK
    visible = (k_chunk >= q_chunk) & (k_chunk <= q_chunk + WINDOW // CHUNK)
    at_start = visible & (k_chunk >= WINDOW // CHUNK)
    return jnp.asarray(np.where(np.stack([visible, at_start]), 0.0, -np.inf), dtype=F32)


def _head_pair_order():
    order = []
    for j in range(ATTN_GROUP):
        for kvh in range(ATTN_KV_HEADS):
            head = kvh * ATTN_GROUP + j
            order.extend(range(head * ATTN_HEAD_DIM, (head + 1) * ATTN_HEAD_DIM))
    return np.asarray(order, dtype=np.int32)


def _layer(x, p, pos, past, w):
    (g_mix_pre, w_in, attn_sinks, w_branch_attn, w_branch_ret, w_out, g_mix_post, g_ffn_pre,
     w_ffn_gate, w_ffn_up, w_ffn_down, g_ffn_post, w_ple_proj, w_ple_gate) = w
    batch, seq_len, d_model = x.shape
    n = batch * seq_len
    x2 = x.reshape(n, d_model)
    p2 = p.reshape(n, p.shape[-1])

    order = _head_pair_order()
    w_in_b = w_in.astype(BF16)
    wq_b = w_in[:, :Q_A][:, order].astype(BF16)
    cos_tab, sin_tab = _rotary_tables(pos)
    g_pre = g_mix_pre.reshape(1, d_model)
    sinks = attn_sinks.astype(F32)
    if past is None and seq_len >= TOKEN_TILE:
        oa, orr, gates, kv32, state = _front_fused(
            x2, g_pre, wq_b, w_in_b, cos_tab, sin_tab, sinks, batch, seq_len)
    else:
        ret_block = min(RET_BLOCK, MIXER_TILE, seq_len)
        z, gates, kv32 = _in_projection(x2, g_pre, wq_b, w_in_b, cos_tab, sin_tab, seq_len, ret_block)
        past_args = None
        if past is not None:
            cache_k, cache_v, state0 = past
            past_args = (cache_k.reshape(batch, WINDOW, KV_A).astype(BF16),
                         cache_v.reshape(batch, WINDOW, KV_A).astype(BF16), state0.astype(F32))
        oa, orr, state = _mixers(z, sinks, batch, seq_len, past_args)

    weights = (w_branch_attn[order, :].astype(BF16), w_branch_ret.astype(BF16), w_out.astype(BF16),
               g_mix_post.reshape(1, d_model), g_ffn_pre.reshape(1, d_model),
               w_ffn_gate.astype(BF16), w_ffn_up.astype(BF16), w_ffn_down.astype(BF16),
               g_ffn_post.reshape(1, d_model), w_ple_proj.astype(BF16), w_ple_gate.astype(BF16))
    y = _output_stage(oa, orr, gates, x2, p2, weights).reshape(batch, seq_len, d_model)

    tail = kv32.shape[0] // batch
    kv32 = kv32.reshape(batch, tail, 2, ATTN_KV_HEADS, ATTN_HEAD_DIM)
    k_new, v_new = kv32[:, :, 0], kv32[:, :, 1]
    if past is not None:
        n_win = past[0].shape[1]
        k_new = jnp.concatenate([past[0], k_new], axis=1)[:, -n_win:]
        v_new = jnp.concatenate([past[1], v_new], axis=1)[:, -n_win:]
    return y, k_new, v_new, state


def kernel(x_prompt, x_sample, p_prompt, p_sample, cache_attn_k, cache_attn_v, state_ret, g_mix_pre, w_in, attn_sinks, w_branch_attn, w_branch_ret, w_out, g_mix_post, g_ffn_pre, w_ffn_gate, w_ffn_up, w_ffn_down, g_ffn_post, w_ple_proj, w_ple_gate):
    depth = w_in.shape[0]
    assert cache_attn_k.shape[2] == WINDOW, "the rolling window must be full"
    pos_prompt = np.arange(x_prompt.shape[1])
    pos_sample = PAST_LEN + np.arange(x_sample.shape[1])
    y_p, y_s = x_prompt, x_sample
    outs = [[] for _ in range(6)]
    for i in range(depth):
        w_i = (g_mix_pre[i], w_in[i], attn_sinks[i], w_branch_attn[i], w_branch_ret[i], w_out[i],
               g_mix_post[i], g_ffn_pre[i], w_ffn_gate[i], w_ffn_up[i], w_ffn_down[i], g_ffn_post[i],
               w_ple_proj[i], w_ple_gate[i])
        y_p, kp, vp, rp = _layer(y_p, p_prompt[i], pos_prompt, None, w_i)
        y_s, ks, vs, rs = _layer(y_s, p_sample[i], pos_sample,
                                 (cache_attn_k[i], cache_attn_v[i], state_ret[i]), w_i)
        for lst, val in zip(outs, (kp, vp, rp.astype(x_prompt.dtype), ks, vs, rs.astype(x_sample.dtype))):
            lst.append(val)
    return (y_p, y_s) + tuple(jnp.stack(lst) for lst in outs)
```

```python
import functools

import jax
import jax.numpy as jnp
import numpy as np
from jax import lax
from jax.experimental import pallas as pl
from jax.experimental.pallas import tpu as pltpu

CHUNK = 64
WINDOW = 128
ATTN_HEADS = 8
ATTN_KV_HEADS = 2
ATTN_GROUP = ATTN_HEADS // ATTN_KV_HEADS
ATTN_HEAD_DIM = 64
RET_HEADS = 4
RET_KEY_DIM = 128
RET_VALUE_DIM = 256
RET_ROPE_BASE = 10000.0
NORM_EPS = 1e-6
GN_EPS = 1e-5
PAST_LEN = 2048

Q_A = ATTN_HEADS * ATTN_HEAD_DIM
KV_A = ATTN_KV_HEADS * ATTN_HEAD_DIM
QK_R = RET_HEADS * RET_KEY_DIM
V_R = RET_HEADS * RET_VALUE_DIM

QA_OFF = 0
KA_OFF = QA_OFF + Q_A
VA_OFF = KA_OFF + KV_A
QR_OFF = VA_OFF + KV_A
KR_OFF = QR_OFF + QK_R
VR_OFF = KR_OFF + QK_R
GR_OFF = VR_OFF + V_R
Z_WIDTH = GR_OFF + V_R

VMEM_LIMIT_BYTES = 56 * 1024 * 1024
TOKEN_TILE = 512
MIXER_TILE = 256
RET_BLOCK = 256
ATT_GROUP = 128
FFN_CHUNK = 512

BF16 = jnp.bfloat16
F32 = jnp.float32
LOG2E = float(np.log2(np.e))


def _dot(a, b):
    return jnp.dot(a, b, preferred_element_type=F32)


def _dot_nt(a, b):
    return lax.dot_general(a, b, (((1,), (1,)), ((), ())), preferred_element_type=F32)


def _dot_tn(a, b):
    return lax.dot_general(a, b, (((0,), (0,)), ((), ())), preferred_element_type=F32)


def _sigmoid(x):
    return 0.5 * jnp.tanh(0.5 * x) + 0.5


def _rmsnorm(x, g):
    return x * lax.rsqrt(jnp.mean(x * x, axis=-1, keepdims=True) + NORM_EPS) * g


def _resident(shape):
    return pl.BlockSpec(shape, lambda *_: (0,) * len(shape), pipeline_mode=pl.Buffered(1))


def _project_qk(x_ref, g_ref, wq_ref, w_ref, cos_ref, sin_ref, qdec_ref, kdec_ref, z_ref):
    h = _rmsnorm(x_ref[...], g_ref[...]).astype(BF16)

    z_ref[:, QA_OFF:QA_OFF + Q_A] = (_dot(h, wq_ref[...]) * (ATTN_HEAD_DIM ** -0.5 * LOG2E)).astype(BF16)
    kv = _dot(h, w_ref[:, KA_OFF:KA_OFF + 2 * KV_A])
    z_ref[:, KA_OFF:KA_OFF + 2 * KV_A] = kv.astype(BF16)

    cos = cos_ref[...]
    sin = sin_ref[...]
    for off, dec_ref in ((QR_OFF, qdec_ref), (KR_OFF, kdec_ref)):
        z = _dot(h, w_ref[:, off:off + QK_R])
        for hh in range(RET_HEADS):
            zh = z[:, hh * RET_KEY_DIM:(hh + 1) * RET_KEY_DIM]
            rot = (zh * cos + pltpu.roll(zh, RET_KEY_DIM // 2, 1) * sin) * dec_ref[hh]
            z_ref[:, off + hh * RET_KEY_DIM:off + (hh + 1) * RET_KEY_DIM] = rot.astype(BF16)
    return h, kv


def _project_rest(h, w_ref, z_ref, gates_ref):
    z_ref[:, VR_OFF:VR_OFF + V_R] = _dot(h, w_ref[:, VR_OFF:VR_OFF + V_R]).astype(BF16)
    g = _dot(h, w_ref[:, GR_OFF:GR_OFF + V_R])
    z_ref[:, GR_OFF:GR_OFF + V_R] = (g * _sigmoid(g)).astype(BF16)
    d_model = gates_ref.shape[1] // 2
    for j in range(2):
        gates_ref[:, j * d_model:(j + 1) * d_model] = _dot(
            h, w_ref[:, Z_WIDTH + j * d_model:Z_WIDTH + (j + 1) * d_model]).astype(BF16)


def _attend(z_ref, sink_ref, bias_ref, oa_ref, kbuf, vbuf, *, tile, att_group, use_bias, seq_start):
    kbuf[WINDOW:, :] = z_ref[:, KA_OFF:KA_OFF + KV_A]
    vbuf[WINDOW:, 0:KV_A] = z_ref[:, VA_OFF:VA_OFF + KV_A]

    low_half = lax.broadcasted_iota(jnp.int32, (att_group, 2 * ATTN_HEAD_DIM), 1) < ATTN_HEAD_DIM
    win = WINDOW + att_group
    for gi in range(tile // att_group):
        rows = slice(gi * att_group, (gi + 1) * att_group)
        q_tiles = [z_ref[rows, QA_OFF + j * 128:QA_OFF + (j + 1) * 128] for j in range(ATTN_GROUP)]
        zero = jnp.zeros_like(q_tiles[0])
        lhs = jnp.concatenate([jnp.where(low_half, q, zero) for q in q_tiles]
                              + [jnp.where(low_half, zero, q) for q in q_tiles], axis=0)
        k_win = kbuf[gi * att_group:gi * att_group + win, :]
        v_win = vbuf[gi * att_group:gi * att_group + win, :]
        s_all = _dot_nt(lhs, k_win)
        if use_bias:
            bias = bias_ref[0] if (seq_start is None or gi > 0) else bias_ref[jnp.where(seq_start, 1, 0)]
        probs, sink_terms = [], []
        for r in range(ATTN_HEADS):
            s = s_all[r * att_group:(r + 1) * att_group]
            if use_bias:
                s = s + bias
            sink = sink_ref[r] * LOG2E
            m = jnp.maximum(jnp.max(s, axis=1, keepdims=True), sink)
            sink_terms.append(jnp.exp2(sink - m))
            probs.append(jnp.exp2(s - m).astype(BF16))
        o = _dot(jnp.concatenate(probs, axis=0), v_win)

        def head_out(r):
            blk = o[r * att_group:(r + 1) * att_group]
            return blk[:, 0:KV_A] * (1.0 / (blk[:, KV_A:] + sink_terms[r]))

        for j in range(ATTN_GROUP):
            oa_ref[rows, j * 128:(j + 1) * 128] = jnp.where(
                low_half, head_out(j), head_out(ATTN_GROUP + j)).astype(BF16)

    kbuf[0:WINDOW, :] = kbuf[tile:tile + WINDOW, :]
    vbuf[0:WINDOW, 0:KV_A] = vbuf[tile:tile + WINDOW, 0:KV_A]


def _retain(z_ref, tril_ref, or_ref, state, *, tile, ret_block, state_decay):
    for bi in range(tile // ret_block):
        rows = slice(bi * ret_block, (bi + 1) * ret_block)
        for hh in range(RET_HEADS):
            vcols = slice(hh * RET_VALUE_DIM, (hh + 1) * RET_VALUE_DIM)
            q = z_ref[rows, QR_OFF + hh * RET_KEY_DIM:QR_OFF + (hh + 1) * RET_KEY_DIM]
            k = z_ref[rows, KR_OFF + hh * RET_KEY_DIM:KR_OFF + (hh + 1) * RET_KEY_DIM]
            v = z_ref[rows, VR_OFF + hh * RET_VALUE_DIM:VR_OFF + (hh + 1) * RET_VALUE_DIM]
            st = state[hh]
            o_r = _dot((_dot_nt(q, k) * tril_ref[...]).astype(BF16), v) + _dot(q, st.astype(BF16))
            state[hh] = state_decay[hh] * (st + _dot_tn(k, v))
            mu = jnp.mean(o_r, axis=1, keepdims=True)
            cen = o_r - mu
            var = jnp.mean(cen * cen, axis=1, keepdims=True)
            gate = z_ref[rows, GR_OFF + hh * RET_VALUE_DIM:GR_OFF + (hh + 1) * RET_VALUE_DIM].astype(F32)
            or_ref[rows, vcols] = (cen * lax.rsqrt(var + GN_EPS) * gate).astype(BF16)


def _init_window(kbuf, vbuf, k_rows, v_rows):
    kbuf[0:WINDOW, :] = k_rows
    vbuf[0:WINDOW, 0:KV_A] = v_rows
    vbuf[:, KV_A:] = jnp.ones((vbuf.shape[0], KV_A), BF16)


def _mix_config(tile):
    ret_block = min(RET_BLOCK, tile)
    att_group = min(ATT_GROUP, tile)
    assert tile % ret_block == 0 and tile % att_group == 0 and att_group % CHUNK == 0
    tril = jnp.asarray(np.tril(np.ones((ret_block, ret_block))), dtype=F32)
    tables = (_attention_bias(att_group), tril)
    att_cfg = dict(tile=tile, att_group=att_group)
    ret_cfg = dict(tile=tile, ret_block=ret_block, state_decay=_state_decay(ret_block))
    return att_cfg, ret_cfg, tables


STATE_SHAPE = (RET_HEADS, RET_KEY_DIM, RET_VALUE_DIM)


def _front_kernel(x_ref, g_ref, wq_ref, w_ref, cos_ref, sin_ref, qdec_ref, kdec_ref, sink_ref, bias_ref, tril_ref,
                  oa_ref, or_ref, gates_ref, kv32_ref, st_out_ref,
                  z_even, z_odd, kbuf, vbuf, state, *, tiles_per_seq, tail_rows, att_cfg, ret_cfg):
    i = pl.program_id(0)
    tile = att_cfg["tile"]
    no_rows = jnp.zeros((WINDOW, KV_A), BF16)

    mix_tile_in_seq = jnp.maximum(i - 1, 0) % tiles_per_seq
    seq_start = mix_tile_in_seq == 0

    @pl.when(i == 0)
    def _():
        z_odd[...] = jnp.zeros(z_odd.shape, BF16)
        vbuf[:, KV_A:] = jnp.ones((vbuf.shape[0], KV_A), BF16)

    @pl.when(seq_start)
    def _():
        kbuf[0:WINDOW, :] = no_rows
        vbuf[0:WINDOW, 0:KV_A] = no_rows
        state[...] = jnp.zeros(state.shape, F32)

    def step(z_write, z_read):
        _retain(z_read, tril_ref, or_ref, state, **ret_cfg)
        h, kv = _project_qk(x_ref, g_ref, wq_ref, w_ref, cos_ref, sin_ref, qdec_ref, kdec_ref, z_write)
        kv32_ref[...] = kv[tile - tail_rows:, :]
        _attend(z_read, sink_ref, bias_ref, oa_ref, kbuf, vbuf, use_bias=True, seq_start=seq_start, **att_cfg)
        _project_rest(h, w_ref, z_write, gates_ref)

    pl.when(i % 2 == 0)(lambda: step(z_even, z_odd))
    pl.when(i % 2 == 1)(lambda: step(z_odd, z_even))

    @pl.when(mix_tile_in_seq == tiles_per_seq - 1)
    def _():
        st_out_ref[0] = state[...]


def _front_fused(x2, g_pre, wq_bf16, w_in_bf16, cos_tab, sin_tab, sinks, batch, seq_len):
    n, d_model = x2.shape
    d_in = w_in_bf16.shape[1]
    tile = TOKEN_TILE
    assert seq_len % tile == 0 and tile >= WINDOW
    tiles_per_seq = seq_len // tile
    n_tiles = n // tile
    tail_rows = WINDOW
    att_cfg, ret_cfg, tables = _mix_config(tile)
    dec_tabs = _projection_decay(tile, ret_cfg["ret_block"])

    def proj_tile(i):
        return jnp.minimum(i, n_tiles - 1)

    def mix_tile(i):
        return jnp.maximum(i - 1, 0)

    def mixed_rows(width):
        return pl.BlockSpec((tile, width), lambda i: (mix_tile(i), 0))

    in_specs = [pl.BlockSpec((tile, d_model), lambda i: (proj_tile(i), 0)),
                _resident((1, d_model)), _resident((d_model, Q_A)), _resident((d_model, d_in)),
                pl.BlockSpec((tile, RET_KEY_DIM), lambda i: (proj_tile(i) % tiles_per_seq, 0)),
                pl.BlockSpec((tile, RET_KEY_DIM), lambda i: (proj_tile(i) % tiles_per_seq, 0)),
                _resident(dec_tabs[0].shape), _resident(dec_tabs[1].shape),
                pl.BlockSpec(memory_space=pltpu.SMEM)]
    in_specs += [_resident(tab.shape) for tab in tables]
    out_specs = [mixed_rows(Q_A), mixed_rows(V_R),
                 pl.BlockSpec((tile, 2 * d_model), lambda i: (proj_tile(i), 0)),
                 pl.BlockSpec((tail_rows, 2 * KV_A), lambda i: (proj_tile(i) // tiles_per_seq, 0)),
                 pl.BlockSpec((1,) + STATE_SHAPE, lambda i: (mix_tile(i) // tiles_per_seq, 0, 0, 0))]
    out_shape = [jax.ShapeDtypeStruct((n, Q_A), BF16), jax.ShapeDtypeStruct((n, V_R), BF16),
                 jax.ShapeDtypeStruct((n, 2 * d_model), BF16),
                 jax.ShapeDtypeStruct((batch * tail_rows, 2 * KV_A), F32),
                 jax.ShapeDtypeStruct((batch,) + STATE_SHAPE, F32)]
    return pl.pallas_call(
        functools.partial(_front_kernel, tiles_per_seq=tiles_per_seq, tail_rows=tail_rows,
                          att_cfg=att_cfg, ret_cfg=ret_cfg),
        grid=(n_tiles + 1,),
        in_specs=in_specs,
        out_specs=out_specs,
        out_shape=out_shape,
        scratch_shapes=[pltpu.VMEM((tile, Z_WIDTH), BF16), pltpu.VMEM((tile, Z_WIDTH), BF16),
                        pltpu.VMEM((WINDOW + tile, KV_A), BF16), pltpu.VMEM((WINDOW + tile, 2 * KV_A), BF16),
                        pltpu.VMEM(STATE_SHAPE, F32)],
        compiler_params=pltpu.CompilerParams(
            dimension_semantics=("arbitrary",), vmem_limit_bytes=VMEM_LIMIT_BYTES),
        name="front",
    )(x2, g_pre, wq_bf16, w_in_bf16, cos_tab, sin_tab, *dec_tabs, sinks, *tables)


def _inproj_kernel(x_ref, g_ref, wq_ref, w_ref, cos_ref, sin_ref, qdec_ref, kdec_ref, z_ref, gates_ref, kv32_ref):
    h, kv = _project_qk(x_ref, g_ref, wq_ref, w_ref, cos_ref, sin_ref, qdec_ref, kdec_ref, z_ref)
    kv32_ref[...] = kv
    _project_rest(h, w_ref, z_ref, gates_ref)


def _in_projection(x2, g_pre, wq_bf16, w_in_bf16, cos_tab, sin_tab, seq_len, ret_block):
    n, d_model = x2.shape
    d_in = w_in_bf16.shape[1]
    tile = min(TOKEN_TILE, n)
    assert n % tile == 0 and tile % seq_len == 0 and seq_len <= WINDOW and seq_len % ret_block == 0
    cos_tab = np.tile(cos_tab, (tile // seq_len, 1))
    sin_tab = np.tile(sin_tab, (tile // seq_len, 1))
    dec_tabs = _projection_decay(tile, ret_block)

    def row_spec(width):
        return pl.BlockSpec((tile, width), lambda i: (i, 0))

    tab_spec = pl.BlockSpec((tile, RET_KEY_DIM), lambda i: (0, 0))
    out_widths = (Z_WIDTH, 2 * d_model)
    out_shape = [jax.ShapeDtypeStruct((n, w), BF16) for w in out_widths]
    out_shape.append(jax.ShapeDtypeStruct((n, 2 * KV_A), F32))
    return pl.pallas_call(
        _inproj_kernel,
        grid=(n // tile,),
        in_specs=[row_spec(d_model), _resident((1, d_model)), _resident((d_model, Q_A)),
                  _resident((d_model, d_in)), tab_spec, tab_spec,
                  _resident(dec_tabs[0].shape), _resident(dec_tabs[1].shape)],
        out_specs=[row_spec(w) for w in out_widths] + [row_spec(2 * KV_A)],
        out_shape=out_shape,
        compiler_params=pltpu.CompilerParams(
            dimension_semantics=("arbitrary",), vmem_limit_bytes=VMEM_LIMIT_BYTES),
        name="in_projection",
    )(x2, g_pre, wq_bf16, w_in_bf16, cos_tab, sin_tab, *dec_tabs)


def _mixer_kernel(*refs, has_past, use_bias, att_cfg, ret_cfg):
    z_ref, sink_ref, bias_ref, tril_ref = refs[:4]
    rest = refs[4:]
    if has_past:
        ck_ref, cv_ref, st0_ref = rest[:3]
        rest = rest[3:]
    oa_ref, or_ref, st_out_ref, kbuf, vbuf, state = rest
    t = pl.program_id(1)

    @pl.when(t == 0)
    def _():
        if has_past:
            _init_window(kbuf, vbuf, ck_ref[0], cv_ref[0])
            state[...] = st0_ref[0]
        else:
            _init_window(kbuf, vbuf, jnp.zeros((WINDOW, KV_A), BF16), jnp.zeros((WINDOW, KV_A), BF16))
            state[...] = jnp.zeros(state.shape, F32)

    _attend(z_ref, sink_ref, bias_ref, oa_ref, kbuf, vbuf,
            use_bias=use_bias, seq_start=None if has_past else t == 0, **att_cfg)
    _retain(z_ref, tril_ref, or_ref, state, **ret_cfg)

    @pl.when(t == pl.num_programs(1) - 1)
    def _():
        st_out_ref[0] = state[...]


def _mixers(z, sinks, batch, seq_len, past=None):
    tile = min(MIXER_TILE, seq_len)
    assert seq_len % tile == 0
    nt = seq_len // tile
    n = batch * seq_len
    att_cfg, ret_cfg, tables = _mix_config(tile)
    use_bias = not (past is not None and att_cfg["att_group"] == CHUNK)

    def row_spec(width):
        return pl.BlockSpec((tile, width), lambda b, t: (b * nt + t, 0))

    in_specs = [row_spec(Z_WIDTH), pl.BlockSpec(memory_space=pltpu.SMEM)]
    in_specs += [_resident(tab.shape) for tab in tables]
    args = [z, sinks] + list(tables)
    if past is not None:
        in_specs += [pl.BlockSpec((1, WINDOW, KV_A), lambda b, t: (b, 0, 0)),
                     pl.BlockSpec((1, WINDOW, KV_A), lambda b, t: (b, 0, 0)),
                     pl.BlockSpec((1,) + STATE_SHAPE, lambda b, t: (b, 0, 0, 0))]
        args += list(past)
    return pl.pallas_call(
        functools.partial(_mixer_kernel, has_past=past is not None, use_bias=use_bias,
                          att_cfg=att_cfg, ret_cfg=ret_cfg),
        grid=(batch, nt),
        in_specs=in_specs,
        out_specs=[row_spec(Q_A), row_spec(V_R),
                   pl.BlockSpec((1,) + STATE_SHAPE, lambda b, t: (b, 0, 0, 0))],
        out_shape=[jax.ShapeDtypeStruct((n, Q_A), BF16), jax.ShapeDtypeStruct((n, V_R), BF16),
                   jax.ShapeDtypeStruct((batch,) + STATE_SHAPE, F32)],
        scratch_shapes=[pltpu.VMEM((WINDOW + tile, KV_A), BF16), pltpu.VMEM((WINDOW + tile, 2 * KV_A), BF16),
                        pltpu.VMEM(STATE_SHAPE, F32)],
        compiler_params=pltpu.CompilerParams(
            dimension_semantics=("arbitrary", "arbitrary"), vmem_limit_bytes=VMEM_LIMIT_BYTES),
        name="mixers",
    )(*args)


def _post_kernel(oa_ref, or_ref, gates_ref, x_ref, p_ref,
                 wba_ref, wbr_ref, wout_ref, gpost_ref, gfpre_ref, wg_ref, wu_ref, wd_ref, gfpost_ref,
                 wpp_ref, wpg_ref, y_ref, *, ffn_slabs):
    d_model = x_ref.shape[1]
    gate_a = gates_ref[:, :d_model].astype(F32)
    gate_r = gates_ref[:, d_model:].astype(F32)
    merged = (_sigmoid(gate_a) * _dot(oa_ref[...], wba_ref[...])
              + _sigmoid(gate_r) * _dot(or_ref[...], wbr_ref[...]))
    y = x_ref[...] + _rmsnorm(_dot(merged.astype(BF16), wout_ref[...]), gpost_ref[...])

    h = (y * gfpre_ref[...]).astype(BF16)
    row_scale = lax.rsqrt(jnp.mean(y * y, axis=-1, keepdims=True) + NORM_EPS)
    f = None
    for lo, hi in ffn_slabs:
        gate = _dot(h, wg_ref[:, lo:hi]) * row_scale
        act = (gate * _sigmoid(gate) * (_dot(h, wu_ref[:, lo:hi]) * row_scale)).astype(BF16)
        part = _dot(act, wd_ref[lo:hi, :])
        f = part if f is None else f + part
    y = y + _rmsnorm(f, gfpost_ref[...])

    emb = _dot(p_ref[...].astype(BF16), wpp_ref[...])
    y_ref[...] = y + emb * _sigmoid(_dot(y.astype(BF16), wpg_ref[...]))


def _output_stage(oa, orr, gates, x2, p2, weights):
    n, d_model = x2.shape
    tile = min(TOKEN_TILE, n)
    assert n % tile == 0
    ffn_hidden = weights[7].shape[0]
    ffn_slabs = tuple((lo, min(lo + FFN_CHUNK, ffn_hidden)) for lo in range(0, ffn_hidden, FFN_CHUNK))

    def row_spec(width):
        return pl.BlockSpec((tile, width), lambda i: (i, 0))

    acts = (oa, orr, gates, x2, p2)
    return pl.pallas_call(
        functools.partial(_post_kernel, ffn_slabs=ffn_slabs),
        grid=(n // tile,),
        in_specs=[row_spec(a.shape[1]) for a in acts] + [_resident(w.shape) for w in weights],
        out_specs=row_spec(d_model),
        out_shape=jax.ShapeDtypeStruct((n, d_model), F32),
        compiler_params=pltpu.CompilerParams(
            dimension_semantics=("arbitrary",), vmem_limit_bytes=VMEM_LIMIT_BYTES),
        name="output_stage",
    )(*acts, *weights)


def _rotary_tables(pos):
    half = RET_KEY_DIM // 2
    inv = 1.0 / (RET_ROPE_BASE ** np.linspace(0.0, 1.0, half))
    ang = np.asarray(pos, np.float64)[:, None] * inv[None, :]
    cos, sin = np.cos(ang), np.sin(ang)
    return (np.concatenate([cos, cos], axis=1).astype(np.float32),
            np.concatenate([-sin, sin], axis=1).astype(np.float32))


def _log_decay():
    return np.log1p(-np.exp2(-5.0 - np.arange(RET_HEADS, dtype=np.float64)))


def _projection_decay(tile, block):
    pos = (np.arange(tile) % block) + 1.0
    q_dec = np.exp(_log_decay()[:, None] * pos[None, :])
    k_dec = np.exp(-_log_decay()[:, None] * pos[None, :]) * (RET_KEY_DIM ** -0.5)
    shape = (RET_HEADS, tile, RET_KEY_DIM)
    return tuple(np.ascontiguousarray(np.broadcast_to(t[:, :, None], shape), dtype=np.float32)
                 for t in (q_dec, k_dec))


def _state_decay(block):
    return tuple(float(v) for v in np.exp(_log_decay() * block))


def _attention_bias(group):
    q_chunk = np.arange(group)[:, None] // CHUNK
    k_chunk = np.arange(WINDOW + group)[None, :] // CHUNK
    visible = (k_chunk >= q_chunk) & (k_chunk <= q_chunk + WINDOW // CHUNK)
    at_start = visible & (k_chunk >= WINDOW // CHUNK)
    return jnp.asarray(np.where(np.stack([visible, at_start]), 0.0, -np.inf), dtype=F32)


def _head_pair_order():
    order = []
    for j in range(ATTN_GROUP):
        for kvh in range(ATTN_KV_HEADS):
            head = kvh * ATTN_GROUP + j
            order.extend(range(head * ATTN_HEAD_DIM, (head + 1) * ATTN_HEAD_DIM))
    return np.asarray(order, dtype=np.int32)


def _layer(x, p, pos, past, w):
    (g_mix_pre, w_in, attn_sinks, w_branch_attn, w_branch_ret, w_out, g_mix_post, g_ffn_pre,
     w_ffn_gate, w_ffn_up, w_ffn_down, g_ffn_post, w_ple_proj, w_ple_gate) = w
    batch, seq_len, d_model = x.shape
    n = batch * seq_len
    x2 = x.reshape(n, d_model)
    p2 = p.reshape(n, p.shape[-1])

    order = _head_pair_order()
    w_in_b = w_in.astype(BF16)
    wq_b = w_in[:, :Q_A][:, order].astype(BF16)
    cos_tab, sin_tab = _rotary_tables(pos)
    g_pre = g_mix_pre.reshape(1, d_model)
    sinks = attn_sinks.astype(F32)
    if past is None and seq_len >= TOKEN_TILE:
        oa, orr, gates, kv32, state = _front_fused(
            x2, g_pre, wq_b, w_in_b, cos_tab, sin_tab, sinks, batch, seq_len)
    else:
        ret_block = min(RET_BLOCK, MIXER_TILE, seq_len)
        z, gates, kv32 = _in_projection(x2, g_pre, wq_b, w_in_b, cos_tab, sin_tab, seq_len, ret_block)
        past_args = None
        if past is not None:
            cache_k, cache_v, state0 = past
            past_args = (cache_k.reshape(batch, WINDOW, KV_A).astype(BF16),
                         cache_v.reshape(batch, WINDOW, KV_A).astype(BF16), state0.astype(F32))
        oa, orr, state = _mixers(z, sinks, batch, seq_len, past_args)

    weights = (w_branch_attn[order, :].astype(BF16), w_branch_ret.astype(BF16), w_out.astype(BF16),
               g_mix_post.reshape(1, d_model), g_ffn_pre.reshape(1, d_model),
               w_ffn_gate.astype(BF16), w_ffn_up.astype(BF16), w_ffn_down.astype(BF16),
               g_ffn_post.reshape(1, d_model), w_ple_proj.astype(BF16), w_ple_gate.astype(BF16))
    y = _output_stage(oa, orr, gates, x2, p2, weights).reshape(batch, seq_len, d_model)

    tail = kv32.shape[0] // batch
    kv32 = kv32.reshape(batch, tail, 2, ATTN_KV_HEADS, ATTN_HEAD_DIM)
    k_new, v_new = kv32[:, :, 0], kv32[:, :, 1]
    if past is not None:
        n_win = past[0].shape[1]
        k_new = jnp.concatenate([past[0], k_new], axis=1)[:, -n_win:]
        v_new = jnp.concatenate([past[1], v_new], axis=1)[:, -n_win:]
    return y, k_new, v_new, state


def kernel(x_prompt, x_sample, p_prompt, p_sample, cache_attn_k, cache_attn_v, state_ret, g_mix_pre, w_in, attn_sinks, w_branch_attn, w_branch_ret, w_out, g_mix_post, g_ffn_pre, w_ffn_gate, w_ffn_up, w_ffn_down, g_ffn_post, w_ple_proj, w_ple_gate):
    depth = w_in.shape[0]
    assert cache_attn_k.shape[2] == WINDOW, "the rolling window must be full"
    pos_prompt = np.arange(x_prompt.shape[1])
    pos_sample = PAST_LEN + np.arange(x_sample.shape[1])
    y_p, y_s = x_prompt, x_sample
    outs = [[] for _ in range(6)]
    for i in range(depth):
        w_i = (g_mix_pre[i], w_in[i], attn_sinks[i], w_branch_attn[i], w_branch_ret[i], w_out[i],
               g_mix_post[i], g_ffn_pre[i], w_ffn_gate[i], w_ffn_up[i], w_ffn_down[i], g_ffn_post[i],
               w_ple_proj[i], w_ple_gate[i])
        y_p, kp, vp, rp = _layer(y_p, p_prompt[i], pos_prompt, None, w_i)
        y_s, ks, vs, rs = _layer(y_s, p_sample[i], pos_sample,
                                 (cache_attn_k[i], cache_attn_v[i], state_ret[i]), w_i)
        for lst, val in zip(outs, (kp, vp, rp.astype(x_prompt.dtype), ks, vs, rs.astype(x_sample.dtype))):
            lst.append(val)
    return (y_p, y_s) + tuple(jnp.stack(lst) for lst in outs)
```

```python
import functools

import jax
import jax.numpy as jnp
import numpy as np
from jax import lax
from jax.experimental import pallas as pl
from jax.experimental.pallas import tpu as pltpu

CHUNK = 64
WINDOW = 128
ATTN_HEADS = 8
ATTN_KV_HEADS = 2
ATTN_GROUP = ATTN_HEADS // ATTN_KV_HEADS
ATTN_HEAD_DIM = 64
RET_HEADS = 4
RET_KEY_DIM = 128
RET_VALUE_DIM = 256
RET_ROPE_BASE = 10000.0
NORM_EPS = 1e-6
GN_EPS = 1e-5
PAST_LEN = 2048

Q_A = ATTN_HEADS * ATTN_HEAD_DIM
HEAD_PAIR = 2 * ATTN_HEAD_DIM
KV_A = ATTN_KV_HEADS * ATTN_HEAD_DIM
QK_R = RET_HEADS * RET_KEY_DIM
V_R = RET_HEADS * RET_VALUE_DIM

QA_OFF = 0
KA_OFF = QA_OFF + Q_A
VA_OFF = KA_OFF + KV_A
QR_OFF = VA_OFF + KV_A
KR_OFF = QR_OFF + QK_R
VR_OFF = KR_OFF + QK_R
GR_OFF = VR_OFF + V_R
Z_WIDTH = GR_OFF + V_R

VMEM_LIMIT_BYTES = 56 * 1024 * 1024
TOKEN_TILE = 512
MIXER_TILE = 256
SHORT_SEQS_PER_STEP = 4
RET_BLOCK = 256
ATT_GROUP = 128
FFN_CHUNK = 512

BF16 = jnp.bfloat16
F32 = jnp.float32
LOG2E = float(np.log2(np.e))


def _dot(a, b):
    return jnp.dot(a, b, preferred_element_type=F32)


def _dot_nt(a, b):
    return lax.dot_general(a, b, (((1,), (1,)), ((), ())), preferred_element_type=F32)


def _dot_tn(a, b):
    return lax.dot_general(a, b, (((0,), (0,)), ((), ())), preferred_element_type=F32)


def _sigmoid(x):
    return 0.5 * jnp.tanh(0.5 * x) + 0.5


def _rmsnorm(x, g):
    return x * lax.rsqrt(jnp.mean(x * x, axis=-1, keepdims=True) + NORM_EPS) * g


def _resident(shape):
    return pl.BlockSpec(shape, lambda *_: (0,) * len(shape), pipeline_mode=pl.Buffered(1))


def _project_qk(x_ref, g_ref, wq_ref, w_ref, cos_ref, sin_ref, qdec_ref, kdec_ref, z_ref):
    h = _rmsnorm(x_ref[...], g_ref[...]).astype(BF16)

    z_ref[:, QA_OFF:QA_OFF + Q_A] = (_dot(h, wq_ref[...]) * (ATTN_HEAD_DIM ** -0.5 * LOG2E)).astype(BF16)
    kv = _dot(h, w_ref[:, KA_OFF:KA_OFF + 2 * KV_A])
    z_ref[:, KA_OFF:KA_OFF + 2 * KV_A] = kv.astype(BF16)

    cos = cos_ref[...]
    sin = sin_ref[...]
    for off, dec_ref in ((QR_OFF, qdec_ref), (KR_OFF, kdec_ref)):
        z = _dot(h, w_ref[:, off:off + QK_R])
        for hh in range(RET_HEADS):
            zh = z[:, hh * RET_KEY_DIM:(hh + 1) * RET_KEY_DIM]
            rot = (zh * cos + pltpu.roll(zh, RET_KEY_DIM // 2, 1) * sin) * dec_ref[hh]
            z_ref[:, off + hh * RET_KEY_DIM:off + (hh + 1) * RET_KEY_DIM] = rot.astype(BF16)
    return h, kv


def _project_rest(h, w_ref, z_ref, gates_ref):
    z_ref[:, VR_OFF:VR_OFF + V_R] = _dot(h, w_ref[:, VR_OFF:VR_OFF + V_R]).astype(BF16)
    g = _dot(h, w_ref[:, GR_OFF:GR_OFF + V_R])
    z_ref[:, GR_OFF:GR_OFF + V_R] = (g * _sigmoid(g)).astype(BF16)
    d_model = gates_ref.shape[1] // 2
    for j in range(2):
        gates_ref[:, j * d_model:(j + 1) * d_model] = _dot(
            h, w_ref[:, Z_WIDTH + j * d_model:Z_WIDTH + (j + 1) * d_model]).astype(BF16)


def _attend(z_ref, sink_ref, bias_ref, oa_ref, kbuf, vbuf, *, tile, att_group, use_bias, seq_start):
    kbuf[WINDOW:, :] = z_ref[:, KA_OFF:KA_OFF + KV_A]
    vbuf[WINDOW:, 0:KV_A] = z_ref[:, VA_OFF:VA_OFF + KV_A]

    low_half = lax.broadcasted_iota(jnp.int32, (att_group, HEAD_PAIR), 1) < ATTN_HEAD_DIM
    win = WINDOW + att_group
    for gi in range(tile // att_group):
        rows = slice(gi * att_group, (gi + 1) * att_group)
        q_tiles = [z_ref[rows, QA_OFF + j * HEAD_PAIR:QA_OFF + (j + 1) * HEAD_PAIR] for j in range(ATTN_GROUP)]
        zero = jnp.zeros_like(q_tiles[0])
        lhs = jnp.concatenate([jnp.where(low_half, q, zero) for q in q_tiles]
                              + [jnp.where(low_half, zero, q) for q in q_tiles], axis=0)
        k_win = kbuf[gi * att_group:gi * att_group + win, :]
        v_win = vbuf[gi * att_group:gi * att_group + win, :]
        s_all = _dot_nt(lhs, k_win)
        if use_bias:
            bias = bias_ref[0] if (seq_start is None or gi > 0) else bias_ref[jnp.where(seq_start, 1, 0)]
        probs, sink_terms = [], []
        for r in range(ATTN_HEADS):
            s = s_all[r * att_group:(r + 1) * att_group]
            if use_bias:
                s = s + bias
            sink = sink_ref[r] * LOG2E
            m = jnp.maximum(jnp.max(s, axis=1, keepdims=True), sink)
            sink_terms.append(jnp.exp2(sink - m))
            probs.append(jnp.exp2(s - m).astype(BF16))
        o = _dot(jnp.concatenate(probs, axis=0), v_win)

        def head_out(r):
            blk = o[r * att_group:(r + 1) * att_group]
            return blk[:, 0:KV_A] * (1.0 / (blk[:, KV_A:] + sink_terms[r]))

        for j in range(ATTN_GROUP):
            oa_ref[rows, j * HEAD_PAIR:(j + 1) * HEAD_PAIR] = jnp.where(
                low_half, head_out(j), head_out(ATTN_GROUP + j)).astype(BF16)

    kbuf[0:WINDOW, :] = kbuf[tile:tile + WINDOW, :]
    vbuf[0:WINDOW, 0:KV_A] = vbuf[tile:tile + WINDOW, 0:KV_A]


def _retain(z_ref, tril_ref, or_ref, state, *, tile, ret_block, state_decay):
    for bi in range(tile // ret_block):
        rows = slice(bi * ret_block, (bi + 1) * ret_block)
        for hh in range(RET_HEADS):
            vcols = slice(hh * RET_VALUE_DIM, (hh + 1) * RET_VALUE_DIM)
            q = z_ref[rows, QR_OFF + hh * RET_KEY_DIM:QR_OFF + (hh + 1) * RET_KEY_DIM]
            k = z_ref[rows, KR_OFF + hh * RET_KEY_DIM:KR_OFF + (hh + 1) * RET_KEY_DIM]
            v = z_ref[rows, VR_OFF + hh * RET_VALUE_DIM:VR_OFF + (hh + 1) * RET_VALUE_DIM]
            st = state[hh]
            o_r = _dot((_dot_nt(q, k) * tril_ref[...]).astype(BF16), v) + _dot(q, st.astype(BF16))
            state[hh] = state_decay[hh] * (st + _dot_tn(k, v))
            mu = jnp.mean(o_r, axis=1, keepdims=True)
            cen = o_r - mu
            var = jnp.mean(cen * cen, axis=1, keepdims=True)
            gate = z_ref[rows, GR_OFF + hh * RET_VALUE_DIM:GR_OFF + (hh + 1) * RET_VALUE_DIM].astype(F32)
            or_ref[rows, vcols] = (cen * lax.rsqrt(var + GN_EPS) * gate).astype(BF16)


def _init_window(kbuf, vbuf, k_rows, v_rows):
    kbuf[0:WINDOW, :] = k_rows
    vbuf[0:WINDOW, 0:KV_A] = v_rows
    vbuf[:, KV_A:] = jnp.ones((vbuf.shape[0], KV_A), BF16)


def _mix_config(tile):
    ret_block = min(RET_BLOCK, tile)
    att_group = min(ATT_GROUP, tile)
    assert tile % ret_block == 0 and tile % att_group == 0 and att_group % CHUNK == 0
    tril = jnp.asarray(np.tril(np.ones((ret_block, ret_block))), dtype=F32)
    tables = (_attention_bias(att_group), tril)
    att_cfg = dict(tile=tile, att_group=att_group)
    ret_cfg = dict(tile=tile, ret_block=ret_block, state_decay=_state_decay(ret_block))
    return att_cfg, ret_cfg, tables


STATE_SHAPE = (RET_HEADS, RET_KEY_DIM, RET_VALUE_DIM)


def _front_kernel(x_ref, g_ref, wq_ref, w_ref, cos_ref, sin_ref, qdec_ref, kdec_ref, sink_ref, bias_ref, tril_ref,
                  oa_ref, or_ref, gates_ref, kv32_ref, st_out_ref,
                  z_even, z_odd, kbuf, vbuf, state, *, tiles_per_seq, tail_rows, att_cfg, ret_cfg):
    i = pl.program_id(0)
    tile = att_cfg["tile"]
    no_rows = jnp.zeros((WINDOW, KV_A), BF16)

    mix_tile_in_seq = jnp.maximum(i - 1, 0) % tiles_per_seq
    seq_start = mix_tile_in_seq == 0

    @pl.when(i == 0)
    def _():
        z_odd[...] = jnp.zeros(z_odd.shape, BF16)
        vbuf[:, KV_A:] = jnp.ones((vbuf.shape[0], KV_A), BF16)

    @pl.when(seq_start)
    def _():
        kbuf[0:WINDOW, :] = no_rows
        vbuf[0:WINDOW, 0:KV_A] = no_rows
        state[...] = jnp.zeros(state.shape, F32)

    def step(z_write, z_read):
        _retain(z_read, tril_ref, or_ref, state, **ret_cfg)
        h, kv = _project_qk(x_ref, g_ref, wq_ref, w_ref, cos_ref, sin_ref, qdec_ref, kdec_ref, z_write)
        kv32_ref[...] = kv[tile - tail_rows:, :]
        _attend(z_read, sink_ref, bias_ref, oa_ref, kbuf, vbuf, use_bias=True, seq_start=seq_start, **att_cfg)
        _project_rest(h, w_ref, z_write, gates_ref)

    pl.when(i % 2 == 0)(lambda: step(z_even, z_odd))
    pl.when(i % 2 == 1)(lambda: step(z_odd, z_even))

    @pl.when(mix_tile_in_seq == tiles_per_seq - 1)
    def _():
        st_out_ref[0] = state[...]


def _front_fused(x2, g_pre, wq_bf16, w_in_bf16, cos_tab, sin_tab, sinks, batch, seq_len):
    n, d_model = x2.shape
    d_in = w_in_bf16.shape[1]
    tile = TOKEN_TILE
    assert seq_len % tile == 0 and tile >= WINDOW
    tiles_per_seq = seq_len // tile
    n_tiles = n // tile
    tail_rows = WINDOW
    att_cfg, ret_cfg, tables = _mix_config(tile)
    dec_tabs = _projection_decay(tile, ret_cfg["ret_block"])

    def proj_tile(i):
        return jnp.minimum(i, n_tiles - 1)

    def mix_tile(i):
        return jnp.maximum(i - 1, 0)

    def mixed_rows(width):
        return pl.BlockSpec((tile, width), lambda i: (mix_tile(i), 0))

    in_specs = [pl.BlockSpec((tile, d_model), lambda i: (proj_tile(i), 0)),
                _resident((1, d_model)), _resident((d_model, Q_A)), _resident((d_model, d_in)),
                pl.BlockSpec((tile, RET_KEY_DIM), lambda i: (proj_tile(i) % tiles_per_seq, 0)),
                pl.BlockSpec((tile, RET_KEY_DIM), lambda i: (proj_tile(i) % tiles_per_seq, 0)),
                _resident(dec_tabs[0].shape), _resident(dec_tabs[1].shape),
                pl.BlockSpec(memory_space=pltpu.SMEM)]
    in_specs += [_resident(tab.shape) for tab in tables]
    out_specs = [mixed_rows(Q_A), mixed_rows(V_R),
                 pl.BlockSpec((tile, 2 * d_model), lambda i: (proj_tile(i), 0)),
                 pl.BlockSpec((tail_rows, 2 * KV_A), lambda i: (proj_tile(i) // tiles_per_seq, 0)),
                 pl.BlockSpec((1,) + STATE_SHAPE, lambda i: (mix_tile(i) // tiles_per_seq, 0, 0, 0))]
    out_shape = [jax.ShapeDtypeStruct((n, Q_A), BF16), jax.ShapeDtypeStruct((n, V_R), BF16),
                 jax.ShapeDtypeStruct((n, 2 * d_model), BF16),
                 jax.ShapeDtypeStruct((batch * tail_rows, 2 * KV_A), F32),
                 jax.ShapeDtypeStruct((batch,) + STATE_SHAPE, F32)]
    return pl.pallas_call(
        functools.partial(_front_kernel, tiles_per_seq=tiles_per_seq, tail_rows=tail_rows,
                          att_cfg=att_cfg, ret_cfg=ret_cfg),
        grid=(n_tiles + 1,),
        in_specs=in_specs,
        out_specs=out_specs,
        out_shape=out_shape,
        scratch_shapes=[pltpu.VMEM((tile, Z_WIDTH), BF16), pltpu.VMEM((tile, Z_WIDTH), BF16),
                        pltpu.VMEM((WINDOW + tile, KV_A), BF16), pltpu.VMEM((WINDOW + tile, 2 * KV_A), BF16),
                        pltpu.VMEM(STATE_SHAPE, F32)],
        compiler_params=pltpu.CompilerParams(
            dimension_semantics=("arbitrary",), vmem_limit_bytes=VMEM_LIMIT_BYTES),
        name="front",
    )(x2, g_pre, wq_bf16, w_in_bf16, cos_tab, sin_tab, *dec_tabs, sinks, *tables)


def _inproj_kernel(x_ref, g_ref, wq_ref, w_ref, cos_ref, sin_ref, qdec_ref, kdec_ref, z_ref, gates_ref, kv32_ref):
    h, kv = _project_qk(x_ref, g_ref, wq_ref, w_ref, cos_ref, sin_ref, qdec_ref, kdec_ref, z_ref)
    kv32_ref[...] = kv
    _project_rest(h, w_ref, z_ref, gates_ref)


def _in_projection(x2, g_pre, wq_bf16, w_in_bf16, cos_tab, sin_tab, seq_len, ret_block):
    n, d_model = x2.shape
    d_in = w_in_bf16.shape[1]
    tile = min(TOKEN_TILE, n)
    assert n % tile == 0 and tile % seq_len == 0 and seq_len <= WINDOW and seq_len % ret_block == 0
    cos_tab = np.tile(cos_tab, (tile // seq_len, 1))
    sin_tab = np.tile(sin_tab, (tile // seq_len, 1))
    dec_tabs = _projection_decay(tile, ret_block)

    def row_spec(width):
        return pl.BlockSpec((tile, width), lambda i: (i, 0))

    tab_spec = pl.BlockSpec((tile, RET_KEY_DIM), lambda i: (0, 0))
    out_widths = (Z_WIDTH, 2 * d_model)
    out_shape = [jax.ShapeDtypeStruct((n, w), BF16) for w in out_widths]
    out_shape.append(jax.ShapeDtypeStruct((n, 2 * KV_A), F32))
    return pl.pallas_call(
        _inproj_kernel,
        grid=(n // tile,),
        in_specs=[row_spec(d_model), _resident((1, d_model)), _resident((d_model, Q_A)),
                  _resident((d_model, d_in)), tab_spec, tab_spec,
                  _resident(dec_tabs[0].shape), _resident(dec_tabs[1].shape)],
        out_specs=[row_spec(w) for w in out_widths] + [row_spec(2 * KV_A)],
        out_shape=out_shape,
        compiler_params=pltpu.CompilerParams(
            dimension_semantics=("arbitrary",), vmem_limit_bytes=VMEM_LIMIT_BYTES),
        name="in_projection",
    )(x2, g_pre, wq_bf16, w_in_bf16, cos_tab, sin_tab, *dec_tabs)


def _mixer_kernel(*refs, seqs, has_past, use_bias, att_cfg, ret_cfg):
    z_ref, sink_ref, bias_ref, tril_ref = refs[:4]
    rest = refs[4:]
    if has_past:
        ck_ref, cv_ref, st0_ref = rest[:3]
        rest = rest[3:]
    oa_ref, or_ref, st_out_ref, kbuf, vbuf, state = rest
    t = pl.program_id(1)
    tile = att_cfg["tile"]

    @pl.when(t == 0)
    def _():
        for s in range(seqs):
            if has_past:
                _init_window(kbuf.at[s], vbuf.at[s], ck_ref[s], cv_ref[s])
                state[s] = st0_ref[s]
            else:
                no_rows = jnp.zeros((WINDOW, KV_A), BF16)
                _init_window(kbuf.at[s], vbuf.at[s], no_rows, no_rows)
                state[s] = jnp.zeros(STATE_SHAPE, F32)

    for s in range(seqs):
        rows = pl.ds(s * tile, tile)
        _attend(z_ref.at[rows, :], sink_ref, bias_ref, oa_ref.at[rows, :], kbuf.at[s], vbuf.at[s],
                use_bias=use_bias, seq_start=None if has_past else t == 0, **att_cfg)
        _retain(z_ref.at[rows, :], tril_ref, or_ref.at[rows, :], state.at[s], **ret_cfg)

    @pl.when(t == pl.num_programs(1) - 1)
    def _():
        st_out_ref[...] = state[...]


def _mixers(z, sinks, batch, seq_len, past=None):
    tile = min(MIXER_TILE, seq_len)
    assert seq_len % tile == 0
    nt = seq_len // tile
    n = batch * seq_len
    seqs = SHORT_SEQS_PER_STEP if nt == 1 and batch % SHORT_SEQS_PER_STEP == 0 else 1
    att_cfg, ret_cfg, tables = _mix_config(tile)
    use_bias = not (past is not None and att_cfg["att_group"] == CHUNK)

    def row_spec(width):
        return pl.BlockSpec((seqs * tile, width), lambda b, t: (b * nt + t, 0))

    def per_seq_spec(shape):
        return pl.BlockSpec((seqs,) + shape, lambda b, t: (b,) + (0,) * len(shape))

    in_specs = [row_spec(Z_WIDTH), pl.BlockSpec(memory_space=pltpu.SMEM)]
    in_specs += [_resident(tab.shape) for tab in tables]
    args = [z, sinks] + list(tables)
    if past is not None:
        in_specs += [per_seq_spec((WINDOW, KV_A)), per_seq_spec((WINDOW, KV_A)), per_seq_spec(STATE_SHAPE)]
        args += list(past)
    return pl.pallas_call(
        functools.partial(_mixer_kernel, seqs=seqs, has_past=past is not None, use_bias=use_bias,
                          att_cfg=att_cfg, ret_cfg=ret_cfg),
        grid=(batch // seqs, nt),
        in_specs=in_specs,
        out_specs=[row_spec(Q_A), row_spec(V_R), per_seq_spec(STATE_SHAPE)],
        out_shape=[jax.ShapeDtypeStruct((n, Q_A), BF16), jax.ShapeDtypeStruct((n, V_R), BF16),
                   jax.ShapeDtypeStruct((batch,) + STATE_SHAPE, F32)],
        scratch_shapes=[pltpu.VMEM((seqs, WINDOW + tile, KV_A), BF16),
                        pltpu.VMEM((seqs, WINDOW + tile, 2 * KV_A), BF16),
                        pltpu.VMEM((seqs,) + STATE_SHAPE, F32)],
        compiler_params=pltpu.CompilerParams(
            dimension_semantics=("arbitrary", "arbitrary"), vmem_limit_bytes=VMEM_LIMIT_BYTES),
        name="mixers",
    )(*args)


def _post_kernel(oa_ref, or_ref, gates_ref, x_ref, p_ref,
                 wba_ref, wbr_ref, wout_ref, gpost_ref, gfpre_ref, wg_ref, wu_ref, wd_ref, gfpost_ref,
                 wpp_ref, wpg_ref, y_ref, *, ffn_slabs):
    d_model = x_ref.shape[1]
    gate_a = gates_ref[:, :d_model].astype(F32)
    gate_r = gates_ref[:, d_model:].astype(F32)
    merged = (_sigmoid(gate_a) * _dot(oa_ref[...], wba_ref[...])
              + _sigmoid(gate_r) * _dot(or_ref[...], wbr_ref[...]))
    y = x_ref[...] + _rmsnorm(_dot(merged.astype(BF16), wout_ref[...]), gpost_ref[...])

    h = (y * gfpre_ref[...]).astype(BF16)
    row_scale = lax.rsqrt(jnp.mean(y * y, axis=-1, keepdims=True) + NORM_EPS)
    f = None
    for lo, hi in ffn_slabs:
        gate = _dot(h, wg_ref[:, lo:hi]) * row_scale
        act = (gate * _sigmoid(gate) * (_dot(h, wu_ref[:, lo:hi]) * row_scale)).astype(BF16)
        part = _dot(act, wd_ref[lo:hi, :])
        f = part if f is None else f + part
    y = y + _rmsnorm(f, gfpost_ref[...])

    emb = _dot(p_ref[...].astype(BF16), wpp_ref[...])
    y_ref[...] = y + emb * _sigmoid(_dot(y.astype(BF16), wpg_ref[...]))


def _output_stage(oa, orr, gates, x2, p2, weights):
    n, d_model = x2.shape
    tile = min(TOKEN_TILE, n)
    assert n % tile == 0
    ffn_hidden = weights[7].shape[0]
    ffn_slabs = tuple((lo, min(lo + FFN_CHUNK, ffn_hidden)) for lo in range(0, ffn_hidden, FFN_CHUNK))

    def row_spec(width):
        return pl.BlockSpec((tile, width), lambda i: (i, 0))

    acts = (oa, orr, gates, x2, p2)
    return pl.pallas_call(
        functools.partial(_post_kernel, ffn_slabs=ffn_slabs),
        grid=(n // tile,),
        in_specs=[row_spec(a.shape[1]) for a in acts] + [_resident(w.shape) for w in weights],
        out_specs=row_spec(d_model),
        out_shape=jax.ShapeDtypeStruct((n, d_model), F32),
        compiler_params=pltpu.CompilerParams(
            dimension_semantics=("arbitrary",), vmem_limit_bytes=VMEM_LIMIT_BYTES),
        name="output_stage",
    )(*acts, *weights)


def _rotary_tables(pos):
    half = RET_KEY_DIM // 2
    inv = 1.0 / (RET_ROPE_BASE ** np.linspace(0.0, 1.0, half))
    ang = np.asarray(pos, np.float64)[:, None] * inv[None, :]
    cos, sin = np.cos(ang), np.sin(ang)
    return (np.concatenate([cos, cos], axis=1).astype(np.float32),
            np.concatenate([-sin, sin], axis=1).astype(np.float32))


def _log_decay():
    return np.log1p(-np.exp2(-5.0 - np.arange(RET_HEADS, dtype=np.float64)))


def _projection_decay(tile, block):
    pos = (np.arange(tile) % block) + 1.0
    q_dec = np.exp(_log_decay()[:, None] * pos[None, :])
    k_dec = np.exp(-_log_decay()[:, None] * pos[None, :]) * (RET_KEY_DIM ** -0.5)
    shape = (RET_HEADS, tile, RET_KEY_DIM)
    return tuple(np.ascontiguousarray(np.broadcast_to(t[:, :, None], shape), dtype=np.float32)
                 for t in (q_dec, k_dec))


def _state_decay(block):
    return tuple(float(v) for v in np.exp(_log_decay() * block))


def _attention_bias(group):
    q_chunk = np.arange(group)[:, None] // CHUNK
    k_chunk = np.arange(WINDOW + group)[None, :] // CHUNK
    visible = (k_chunk >= q_chunk) & (k_chunk <= q_chunk + WINDOW // CHUNK)
    at_start = visible & (k_chunk >= WINDOW // CHUNK)
    return jnp.asarray(np.where(np.stack([visible, at_start]), 0.0, -np.inf), dtype=F32)


def _head_pair_order():
    order = []
    for j in range(ATTN_GROUP):
        for kvh in range(ATTN_KV_HEADS):
            head = kvh * ATTN_GROUP + j
            order.extend(range(head * ATTN_HEAD_DIM, (head + 1) * ATTN_HEAD_DIM))
    return np.asarray(order, dtype=np.int32)


def _layer(x, p, pos, past, w):
    (g_mix_pre, w_in, attn_sinks, w_branch_attn, w_branch_ret, w_out, g_mix_post, g_ffn_pre,
     w_ffn_gate, w_ffn_up, w_ffn_down, g_ffn_post, w_ple_proj, w_ple_gate) = w
    batch, seq_len, d_model = x.shape
    n = batch * seq_len
    x2 = x.reshape(n, d_model)
    p2 = p.reshape(n, p.shape[-1])

    order = _head_pair_order()
    w_in_b = w_in.astype(BF16)
    wq_b = w_in[:, :Q_A][:, order].astype(BF16)
    cos_tab, sin_tab = _rotary_tables(pos)
    g_pre = g_mix_pre.reshape(1, d_model)
    sinks = attn_sinks.astype(F32)
    if past is None and seq_len >= TOKEN_TILE:
        oa, orr, gates, kv32, state = _front_fused(
            x2, g_pre, wq_b, w_in_b, cos_tab, sin_tab, sinks, batch, seq_len)
    else:
        ret_block = min(RET_BLOCK, MIXER_TILE, seq_len)
        z, gates, kv32 = _in_projection(x2, g_pre, wq_b, w_in_b, cos_tab, sin_tab, seq_len, ret_block)
        past_args = None
        if past is not None:
            cache_k, cache_v, state0 = past
            past_args = (cache_k.reshape(batch, WINDOW, KV_A).astype(BF16),
                         cache_v.reshape(batch, WINDOW, KV_A).astype(BF16), state0.astype(F32))
        oa, orr, state = _mixers(z, sinks, batch, seq_len, past_args)

    weights = (w_branch_attn[order, :].astype(BF16), w_branch_ret.astype(BF16), w_out.astype(BF16),
               g_mix_post.reshape(1, d_model), g_ffn_pre.reshape(1, d_model),
               w_ffn_gate.astype(BF16), w_ffn_up.astype(BF16), w_ffn_down.astype(BF16),
               g_ffn_post.reshape(1, d_model), w_ple_proj.astype(BF16), w_ple_gate.astype(BF16))
    y = _output_stage(oa, orr, gates, x2, p2, weights).reshape(batch, seq_len, d_model)

    tail = kv32.shape[0] // batch
    kv32 = kv32.reshape(batch, tail, 2, ATTN_KV_HEADS, ATTN_HEAD_DIM)
    k_new, v_new = kv32[:, :, 0], kv32[:, :, 1]
    if past is not None:
        n_win = past[0].shape[1]
        k_new = jnp.concatenate([past[0], k_new], axis=1)[:, -n_win:]
        v_new = jnp.concatenate([past[1], v_new], axis=1)[:, -n_win:]
    return y, k_new, v_new, state


def kernel(x_prompt, x_sample, p_prompt, p_sample, cache_attn_k, cache_attn_v, state_ret, g_mix_pre, w_in, attn_sinks, w_branch_attn, w_branch_ret, w_out, g_mix_post, g_ffn_pre, w_ffn_gate, w_ffn_up, w_ffn_down, g_ffn_post, w_ple_proj, w_ple_gate):
    depth = w_in.shape[0]
    assert cache_attn_k.shape[2] == WINDOW, "the rolling window must be full"
    pos_prompt = np.arange(x_prompt.shape[1])
    pos_sample = PAST_LEN + np.arange(x_sample.shape[1])
    y_p, y_s = x_prompt, x_sample
    outs = [[] for _ in range(6)]
    for i in range(depth):
        w_i = (g_mix_pre[i], w_in[i], attn_sinks[i], w_branch_attn[i], w_branch_ret[i], w_out[i],
               g_mix_post[i], g_ffn_pre[i], w_ffn_gate[i], w_ffn_up[i], w_ffn_down[i], g_ffn_post[i],
               w_ple_proj[i], w_ple_gate[i])
        y_p, kp, vp, rp = _layer(y_p, p_prompt[i], pos_prompt, None, w_i)
        y_s, ks, vs, rs = _layer(y_s, p_sample[i], pos_sample,
                                 (cache_attn_k[i], cache_attn_v[i], state_ret[i]), w_i)
        for lst, val in zip(outs, (kp, vp, rp.astype(x_prompt.dtype), ks, vs, rs.astype(x_sample.dtype))):
            lst.append(val)
    return (y_p, y_s) + tuple(jnp.stack(lst) for lst in outs)
```

```python
import functools

import jax
import jax.numpy as jnp
import numpy as np
from jax import lax
from jax.experimental import pallas as pl
from jax.experimental.pallas import tpu as pltpu

CHUNK = 64
WINDOW = 128
ATTN_HEADS = 8
ATTN_KV_HEADS = 2
ATTN_GROUP = ATTN_HEADS // ATTN_KV_HEADS
ATTN_HEAD_DIM = 64
RET_HEADS = 4
RET_KEY_DIM = 128
RET_VALUE_DIM = 256
RET_ROPE_BASE = 10000.0
NORM_EPS = 1e-6
GN_EPS = 1e-5
PAST_LEN = 2048

Q_A = ATTN_HEADS * ATTN_HEAD_DIM
HEAD_PAIR = 2 * ATTN_HEAD_DIM
KV_A = ATTN_KV_HEADS * ATTN_HEAD_DIM
QK_R = RET_HEADS * RET_KEY_DIM
V_R = RET_HEADS * RET_VALUE_DIM

QA_OFF = 0
KA_OFF = QA_OFF + Q_A
VA_OFF = KA_OFF + KV_A
QR_OFF = VA_OFF + KV_A
KR_OFF = QR_OFF + QK_R
VR_OFF = KR_OFF + QK_R
GR_OFF = VR_OFF + V_R
Z_WIDTH = GR_OFF + V_R

VMEM_LIMIT_BYTES = 56 * 1024 * 1024
TOKEN_TILE = 512
MIXER_TILE = 256
SHORT_SEQS_PER_STEP = 4
RET_BLOCK = 256
ATT_GROUP = 128
FFN_CHUNK = 512

BF16 = jnp.bfloat16
F32 = jnp.float32
LOG2E = float(np.log2(np.e))


def _dot(a, b):
    return jnp.dot(a, b, preferred_element_type=F32)


def _dot_nt(a, b):
    return lax.dot_general(a, b, (((1,), (1,)), ((), ())), preferred_element_type=F32)


def _dot_tn(a, b):
    return lax.dot_general(a, b, (((0,), (0,)), ((), ())), preferred_element_type=F32)


def _sigmoid(x):
    return 0.5 * jnp.tanh(0.5 * x) + 0.5


def _rmsnorm(x, g):
    return x * lax.rsqrt(jnp.mean(x * x, axis=-1, keepdims=True) + NORM_EPS) * g


def _resident(shape):
    return pl.BlockSpec(shape, lambda *_: (0,) * len(shape), pipeline_mode=pl.Buffered(1))


def _project_qk(x_ref, g_ref, wq_ref, w_ref, cos_ref, sin_ref, qdec_ref, kdec_ref, z_ref):
    h = _rmsnorm(x_ref[...], g_ref[...]).astype(BF16)

    z_ref[:, QA_OFF:QA_OFF + Q_A] = (_dot(h, wq_ref[...]) * (ATTN_HEAD_DIM ** -0.5 * LOG2E)).astype(BF16)
    kv = _dot(h, w_ref[:, KA_OFF:KA_OFF + 2 * KV_A])
    z_ref[:, KA_OFF:KA_OFF + 2 * KV_A] = kv.astype(BF16)

    cos = cos_ref[...]
    sin = sin_ref[...]
    for off, dec_ref in ((QR_OFF, qdec_ref), (KR_OFF, kdec_ref)):
        z = _dot(h, w_ref[:, off:off + QK_R])
        for hh in range(RET_HEADS):
            zh = z[:, hh * RET_KEY_DIM:(hh + 1) * RET_KEY_DIM]
            rot = (zh * cos + pltpu.roll(zh, RET_KEY_DIM // 2, 1) * sin) * dec_ref[hh]
            z_ref[:, off + hh * RET_KEY_DIM:off + (hh + 1) * RET_KEY_DIM] = rot.astype(BF16)
    return h, kv


def _project_rest(h, w_ref, z_ref, gates_ref):
    z_ref[:, VR_OFF:VR_OFF + V_R] = _dot(h, w_ref[:, VR_OFF:VR_OFF + V_R]).astype(BF16)
    g = _dot(h, w_ref[:, GR_OFF:GR_OFF + V_R])
    z_ref[:, GR_OFF:GR_OFF + V_R] = (g * _sigmoid(g)).astype(BF16)
    d_model = gates_ref.shape[1] // 2
    for j in range(2):
        gates_ref[:, j * d_model:(j + 1) * d_model] = _dot(
            h, w_ref[:, Z_WIDTH + j * d_model:Z_WIDTH + (j + 1) * d_model]).astype(BF16)


def _attend(z_ref, sink_ref, bias_ref, oa_ref, kbuf, vbuf, *, tile, att_group, use_bias, seq_start):
    kbuf[WINDOW:, :] = z_ref[:, KA_OFF:KA_OFF + KV_A]
    vbuf[WINDOW:, 0:KV_A] = z_ref[:, VA_OFF:VA_OFF + KV_A]

    low_half = lax.broadcasted_iota(jnp.int32, (att_group, HEAD_PAIR), 1) < ATTN_HEAD_DIM
    win = WINDOW + att_group
    for gi in range(tile // att_group):
        rows = slice(gi * att_group, (gi + 1) * att_group)
        q_tiles = [z_ref[rows, QA_OFF + j * HEAD_PAIR:QA_OFF + (j + 1) * HEAD_PAIR] for j in range(ATTN_GROUP)]
        zero = jnp.zeros_like(q_tiles[0])
        lhs = jnp.concatenate([jnp.where(low_half, q, zero) for q in q_tiles]
                              + [jnp.where(low_half, zero, q) for q in q_tiles], axis=0)
        k_win = kbuf[gi * att_group:gi * att_group + win, :]
        v_win = vbuf[gi * att_group:gi * att_group + win, :]
        s_all = _dot_nt(lhs, k_win)
        if use_bias:
            bias = bias_ref[0] if (seq_start is None or gi > 0) else bias_ref[jnp.where(seq_start, 1, 0)]
        probs, sink_terms = [], []
        for r in range(ATTN_HEADS):
            s = s_all[r * att_group:(r + 1) * att_group]
            if use_bias:
                s = s + bias
            sink = sink_ref[r] * LOG2E
            m = jnp.maximum(jnp.max(s, axis=1, keepdims=True), sink)
            sink_terms.append(jnp.exp2(sink - m))
            probs.append(jnp.exp2(s - m).astype(BF16))
        o = _dot(jnp.concatenate(probs, axis=0), v_win)

        def head_out(r):
            blk = o[r * att_group:(r + 1) * att_group]
            return blk[:, 0:KV_A] * (1.0 / (blk[:, KV_A:] + sink_terms[r]))

        for j in range(ATTN_GROUP):
            oa_ref[rows, j * HEAD_PAIR:(j + 1) * HEAD_PAIR] = jnp.where(
                low_half, head_out(j), head_out(ATTN_GROUP + j)).astype(BF16)

    kbuf[0:WINDOW, :] = kbuf[tile:tile + WINDOW, :]
    vbuf[0:WINDOW, 0:KV_A] = vbuf[tile:tile + WINDOW, 0:KV_A]


def _retain(z_ref, tril_ref, or_ref, state, *, tile, ret_block, state_decay):
    for bi in range(tile // ret_block):
        rows = slice(bi * ret_block, (bi + 1) * ret_block)
        for hh in range(RET_HEADS):
            vcols = slice(hh * RET_VALUE_DIM, (hh + 1) * RET_VALUE_DIM)
            q = z_ref[rows, QR_OFF + hh * RET_KEY_DIM:QR_OFF + (hh + 1) * RET_KEY_DIM]
            k = z_ref[rows, KR_OFF + hh * RET_KEY_DIM:KR_OFF + (hh + 1) * RET_KEY_DIM]
            v = z_ref[rows, VR_OFF + hh * RET_VALUE_DIM:VR_OFF + (hh + 1) * RET_VALUE_DIM]
            st = state[hh]
            o_r = _dot((_dot_nt(q, k) * tril_ref[...]).astype(BF16), v) + _dot(q, st.astype(BF16))
            state[hh] = state_decay[hh] * (st + _dot_tn(k, v))
            mu = jnp.mean(o_r, axis=1, keepdims=True)
            cen = o_r - mu
            var = jnp.mean(cen * cen, axis=1, keepdims=True)
            gate = z_ref[rows, GR_OFF + hh * RET_VALUE_DIM:GR_OFF + (hh + 1) * RET_VALUE_DIM].astype(F32)
            or_ref[rows, vcols] = (cen * lax.rsqrt(var + GN_EPS) * gate).astype(BF16)


def _init_window(kbuf, vbuf, k_rows, v_rows):
    kbuf[0:WINDOW, :] = k_rows
    vbuf[0:WINDOW, 0:KV_A] = v_rows
    vbuf[:, KV_A:] = jnp.ones((vbuf.shape[0], KV_A), BF16)


def _mix_config(tile):
    ret_block = min(RET_BLOCK, tile)
    att_group = min(ATT_GROUP, tile)
    assert tile % ret_block == 0 and tile % att_group == 0 and att_group % CHUNK == 0
    tril = jnp.asarray(np.tril(np.ones((ret_block, ret_block))), dtype=F32)
    tables = (_attention_bias(att_group), tril)
    att_cfg = dict(tile=tile, att_group=att_group)
    ret_cfg = dict(tile=tile, ret_block=ret_block, state_decay=_state_decay(ret_block))
    return att_cfg, ret_cfg, tables


STATE_SHAPE = (RET_HEADS, RET_KEY_DIM, RET_VALUE_DIM)


def _front_kernel(x_ref, g_ref, wq_ref, w_ref, cos_ref, sin_ref, qdec_ref, kdec_ref, sink_ref, bias_ref, tril_ref,
                  oa_ref, or_ref, gates_ref, kv32_ref, st_out_ref,
                  z_even, z_odd, kbuf, vbuf, state, *, tiles_per_seq, tail_rows, att_cfg, ret_cfg):
    i = pl.program_id(0)
    tile = att_cfg["tile"]
    no_rows = jnp.zeros((WINDOW, KV_A), BF16)

    mix_tile_in_seq = jnp.maximum(i - 1, 0) % tiles_per_seq
    seq_start = mix_tile_in_seq == 0

    @pl.when(i == 0)
    def _():
        z_odd[...] = jnp.zeros(z_odd.shape, BF16)
        vbuf[:, KV_A:] = jnp.ones((vbuf.shape[0], KV_A), BF16)

    @pl.when(seq_start)
    def _():
        kbuf[0:WINDOW, :] = no_rows
        vbuf[0:WINDOW, 0:KV_A] = no_rows
        state[...] = jnp.zeros(state.shape, F32)

    def step(z_write, z_read):
        _retain(z_read, tril_ref, or_ref, state, **ret_cfg)
        h, kv = _project_qk(x_ref, g_ref, wq_ref, w_ref, cos_ref, sin_ref, qdec_ref, kdec_ref, z_write)
        kv32_ref[...] = kv[tile - tail_rows:, :]
        _attend(z_read, sink_ref, bias_ref, oa_ref, kbuf, vbuf, use_bias=True, seq_start=seq_start, **att_cfg)
        _project_rest(h, w_ref, z_write, gates_ref)

    pl.when(i % 2 == 0)(lambda: step(z_even, z_odd))
    pl.when(i % 2 == 1)(lambda: step(z_odd, z_even))

    @pl.when(mix_tile_in_seq == tiles_per_seq - 1)
    def _():
        st_out_ref[0] = state[...]


def _front_fused(x2, g_pre, wq_bf16, w_in_bf16, cos_tab, sin_tab, sinks, batch, seq_len):
    n, d_model = x2.shape
    d_in = w_in_bf16.shape[1]
    tile = TOKEN_TILE
    assert seq_len % tile == 0 and tile >= WINDOW
    tiles_per_seq = seq_len // tile
    n_tiles = n // tile
    tail_rows = WINDOW
    att_cfg, ret_cfg, tables = _mix_config(tile)
    dec_tabs = _projection_decay(tile, ret_cfg["ret_block"])

    def proj_tile(i):
        return jnp.minimum(i, n_tiles - 1)

    def mix_tile(i):
        return jnp.maximum(i - 1, 0)

    def mixed_rows(width):
        return pl.BlockSpec((tile, width), lambda i: (mix_tile(i), 0))

    in_specs = [pl.BlockSpec((tile, d_model), lambda i: (proj_tile(i), 0)),
                _resident((1, d_model)), _resident((d_model, Q_A)), _resident((d_model, d_in)),
                pl.BlockSpec((tile, RET_KEY_DIM), lambda i: (proj_tile(i) % tiles_per_seq, 0)),
                pl.BlockSpec((tile, RET_KEY_DIM), lambda i: (proj_tile(i) % tiles_per_seq, 0)),
                _resident(dec_tabs[0].shape), _resident(dec_tabs[1].shape),
                pl.BlockSpec(memory_space=pltpu.SMEM)]
    in_specs += [_resident(tab.shape) for tab in tables]
    out_specs = [mixed_rows(Q_A), mixed_rows(V_R),
                 pl.BlockSpec((tile, 2 * d_model), lambda i: (proj_tile(i), 0)),
                 pl.BlockSpec((tail_rows, 2 * KV_A), lambda i: (proj_tile(i) // tiles_per_seq, 0)),
                 pl.BlockSpec((1,) + STATE_SHAPE, lambda i: (mix_tile(i) // tiles_per_seq, 0, 0, 0))]
    out_shape = [jax.ShapeDtypeStruct((n, Q_A), BF16), jax.ShapeDtypeStruct((n, V_R), BF16),
                 jax.ShapeDtypeStruct((n, 2 * d_model), BF16),
                 jax.ShapeDtypeStruct((batch * tail_rows, 2 * KV_A), F32),
                 jax.ShapeDtypeStruct((batch,) + STATE_SHAPE, F32)]
    return pl.pallas_call(
        functools.partial(_front_kernel, tiles_per_seq=tiles_per_seq, tail_rows=tail_rows,
                          att_cfg=att_cfg, ret_cfg=ret_cfg),
        grid=(n_tiles + 1,),
        in_specs=in_specs,
        out_specs=out_specs,
        out_shape=out_shape,
        scratch_shapes=[pltpu.VMEM((tile, Z_WIDTH), BF16), pltpu.VMEM((tile, Z_WIDTH), BF16),
                        pltpu.VMEM((WINDOW + tile, KV_A), BF16), pltpu.VMEM((WINDOW + tile, 2 * KV_A), BF16),
                        pltpu.VMEM(STATE_SHAPE, F32)],
        compiler_params=pltpu.CompilerParams(
            dimension_semantics=("arbitrary",), vmem_limit_bytes=VMEM_LIMIT_BYTES),
        name="front",
    )(x2, g_pre, wq_bf16, w_in_bf16, cos_tab, sin_tab, *dec_tabs, sinks, *tables)


def _inproj_kernel(x_ref, g_ref, wq_ref, w_ref, cos_ref, sin_ref, qdec_ref, kdec_ref, z_ref, gates_ref, kv32_ref):
    h, kv = _project_qk(x_ref, g_ref, wq_ref, w_ref, cos_ref, sin_ref, qdec_ref, kdec_ref, z_ref)
    kv32_ref[...] = kv
    _project_rest(h, w_ref, z_ref, gates_ref)


def _in_projection(x2, g_pre, wq_bf16, w_in_bf16, cos_tab, sin_tab, seq_len, ret_block):
    n, d_model = x2.shape
    d_in = w_in_bf16.shape[1]
    tile = min(TOKEN_TILE, n)
    assert n % tile == 0 and tile % seq_len == 0 and seq_len <= WINDOW and seq_len % ret_block == 0
    cos_tab = np.tile(cos_tab, (tile // seq_len, 1))
    sin_tab = np.tile(sin_tab, (tile // seq_len, 1))
    dec_tabs = _projection_decay(tile, ret_block)

    def row_spec(width):
        return pl.BlockSpec((tile, width), lambda i: (i, 0))

    tab_spec = pl.BlockSpec((tile, RET_KEY_DIM), lambda i: (0, 0))
    out_widths = (Z_WIDTH, 2 * d_model)
    out_shape = [jax.ShapeDtypeStruct((n, w), BF16) for w in out_widths]
    out_shape.append(jax.ShapeDtypeStruct((n, 2 * KV_A), F32))
    return pl.pallas_call(
        _inproj_kernel,
        grid=(n // tile,),
        in_specs=[row_spec(d_model), _resident((1, d_model)), _resident((d_model, Q_A)),
                  _resident((d_model, d_in)), tab_spec, tab_spec,
                  _resident(dec_tabs[0].shape), _resident(dec_tabs[1].shape)],
        out_specs=[row_spec(w) for w in out_widths] + [row_spec(2 * KV_A)],
        out_shape=out_shape,
        compiler_params=pltpu.CompilerParams(
            dimension_semantics=("arbitrary",), vmem_limit_bytes=VMEM_LIMIT_BYTES),
        name="in_projection",
    )(x2, g_pre, wq_bf16, w_in_bf16, cos_tab, sin_tab, *dec_tabs)


def _mixer_kernel(*refs, seqs, has_past, use_bias, att_cfg, ret_cfg):
    z_ref, sink_ref, bias_ref, tril_ref = refs[:4]
    rest = refs[4:]
    if has_past:
        ck_ref, cv_ref, st0_ref = rest[:3]
        rest = rest[3:]
    oa_ref, or_ref, st_out_ref, kbuf, vbuf, state = rest
    t = pl.program_id(1)
    tile = att_cfg["tile"]

    @pl.when(t == 0)
    def _():
        for s in range(seqs):
            if has_past:
                _init_window(kbuf.at[s], vbuf.at[s], ck_ref[s], cv_ref[s])
                state[s] = st0_ref[s]
            else:
                no_rows = jnp.zeros((WINDOW, KV_A), BF16)
                _init_window(kbuf.at[s], vbuf.at[s], no_rows, no_rows)
                state[s] = jnp.zeros(STATE_SHAPE, F32)

    for s in range(seqs):
        rows = pl.ds(s * tile, tile)
        _attend(z_ref.at[rows, :], sink_ref, bias_ref, oa_ref.at[rows, :], kbuf.at[s], vbuf.at[s],
                use_bias=use_bias, seq_start=None if has_past else t == 0, **att_cfg)
        _retain(z_ref.at[rows, :], tril_ref, or_ref.at[rows, :], state.at[s], **ret_cfg)

    @pl.when(t == pl.num_programs(1) - 1)
    def _():
        st_out_ref[...] = state[...]


def _mixers(z, sinks, batch, seq_len, past=None):
    tile = min(MIXER_TILE, seq_len)
    assert seq_len % tile == 0
    nt = seq_len // tile
    n = batch * seq_len
    seqs = SHORT_SEQS_PER_STEP if nt == 1 and batch % SHORT_SEQS_PER_STEP == 0 else 1
    att_cfg, ret_cfg, tables = _mix_config(tile)
    use_bias = not (past is not None and att_cfg["att_group"] == CHUNK)

    def row_spec(width):
        return pl.BlockSpec((seqs * tile, width), lambda b, t: (b * nt + t, 0))

    def per_seq_spec(shape):
        return pl.BlockSpec((seqs,) + shape, lambda b, t: (b,) + (0,) * len(shape))

    in_specs = [row_spec(Z_WIDTH), pl.BlockSpec(memory_space=pltpu.SMEM)]
    in_specs += [_resident(tab.shape) for tab in tables]
    args = [z, sinks] + list(tables)
    if past is not None:
        in_specs += [per_seq_spec((WINDOW, KV_A)), per_seq_spec((WINDOW, KV_A)), per_seq_spec(STATE_SHAPE)]
        args += list(past)
    return pl.pallas_call(
        functools.partial(_mixer_kernel, seqs=seqs, has_past=past is not None, use_bias=use_bias,
                          att_cfg=att_cfg, ret_cfg=ret_cfg),
        grid=(batch // seqs, nt),
        in_specs=in_specs,
        out_specs=[row_spec(Q_A), row_spec(V_R), per_seq_spec(STATE_SHAPE)],
        out_shape=[jax.ShapeDtypeStruct((n, Q_A), BF16), jax.ShapeDtypeStruct((n, V_R), BF16),
                   jax.ShapeDtypeStruct((batch,) + STATE_SHAPE, F32)],
        scratch_shapes=[pltpu.VMEM((seqs, WINDOW + tile, KV_A), BF16),
                        pltpu.VMEM((seqs, WINDOW + tile, 2 * KV_A), BF16),
                        pltpu.VMEM((seqs,) + STATE_SHAPE, F32)],
        compiler_params=pltpu.CompilerParams(
            dimension_semantics=("arbitrary", "arbitrary"), vmem_limit_bytes=VMEM_LIMIT_BYTES),
        name="mixers",
    )(*args)


def _post_kernel(oa_ref, or_ref, gates_ref, x_ref, p_ref,
                 wba_ref, wbr_ref, wout_ref, gpost_ref, gfpre_ref, wg_ref, wu_ref, wd_ref, gfpost_ref,
                 wpp_ref, wpg_ref, y_ref, *, ffn_slabs):
    d_model = x_ref.shape[1]
    gate_a = gates_ref[:, :d_model].astype(F32)
    gate_r = gates_ref[:, d_model:].astype(F32)
    merged = (_sigmoid(gate_a) * _dot(oa_ref[...], wba_ref[...])
              + _sigmoid(gate_r) * _dot(or_ref[...], wbr_ref[...]))
    y = x_ref[...] + _rmsnorm(_dot(merged.astype(BF16), wout_ref[...]), gpost_ref[...])

    h = (y * gfpre_ref[...]).astype(BF16)
    row_scale = lax.rsqrt(jnp.mean(y * y, axis=-1, keepdims=True) + NORM_EPS)
    f = None
    for lo, hi in ffn_slabs:
        gate = _dot(h, wg_ref[:, lo:hi]) * row_scale
        act = (gate * _sigmoid(gate) * (_dot(h, wu_ref[:, lo:hi]) * row_scale)).astype(BF16)
        part = _dot(act, wd_ref[lo:hi, :])
        f = part if f is None else f + part
    y = y + _rmsnorm(f, gfpost_ref[...])

    emb = _dot(p_ref[...].astype(BF16), wpp_ref[...])
    y = y + emb * _sigmoid(_dot(y.astype(BF16), wpg_ref[...]))
    y_ref[...] = y.reshape(y_ref.shape)


def _output_stage(oa, orr, gates, x2, p2, weights, batch, seq_len):
    n, d_model = x2.shape
    tile = min(TOKEN_TILE, n)
    assert n % tile == 0
    ffn_hidden = weights[7].shape[0]
    ffn_slabs = tuple((lo, min(lo + FFN_CHUNK, ffn_hidden)) for lo in range(0, ffn_hidden, FFN_CHUNK))

    def row_spec(width):
        return pl.BlockSpec((tile, width), lambda i: (i, 0))

    if seq_len >= tile:
        assert seq_len % tile == 0
        tiles_per_seq = seq_len // tile
        out_spec = pl.BlockSpec((1, tile, d_model), lambda i: (i // tiles_per_seq, i % tiles_per_seq, 0))
    else:
        assert tile % seq_len == 0
        out_spec = pl.BlockSpec((tile // seq_len, seq_len, d_model), lambda i: (i, 0, 0))

    acts = (oa, orr, gates, x2, p2)
    return pl.pallas_call(
        functools.partial(_post_kernel, ffn_slabs=ffn_slabs),
        grid=(n // tile,),
        in_specs=[row_spec(a.shape[1]) for a in acts] + [_resident(w.shape) for w in weights],
        out_specs=out_spec,
        out_shape=jax.ShapeDtypeStruct((batch, seq_len, d_model), F32),
        compiler_params=pltpu.CompilerParams(
            dimension_semantics=("arbitrary",), vmem_limit_bytes=VMEM_LIMIT_BYTES),
        name="output_stage",
    )(*acts, *weights)


def _rotary_tables(pos):
    half = RET_KEY_DIM // 2
    inv = 1.0 / (RET_ROPE_BASE ** np.linspace(0.0, 1.0, half))
    ang = np.asarray(pos, np.float64)[:, None] * inv[None, :]
    cos, sin = np.cos(ang), np.sin(ang)
    return (np.concatenate([cos, cos], axis=1).astype(np.float32),
            np.concatenate([-sin, sin], axis=1).astype(np.float32))


def _log_decay():
    return np.log1p(-np.exp2(-5.0 - np.arange(RET_HEADS, dtype=np.float64)))


def _projection_decay(tile, block):
    pos = (np.arange(tile) % block) + 1.0
    q_dec = np.exp(_log_decay()[:, None] * pos[None, :])
    k_dec = np.exp(-_log_decay()[:, None] * pos[None, :]) * (RET_KEY_DIM ** -0.5)
    shape = (RET_HEADS, tile, RET_KEY_DIM)
    return tuple(np.ascontiguousarray(np.broadcast_to(t[:, :, None], shape), dtype=np.float32)
                 for t in (q_dec, k_dec))


def _state_decay(block):
    return tuple(float(v) for v in np.exp(_log_decay() * block))


def _attention_bias(group):
    q_chunk = np.arange(group)[:, None] // CHUNK
    k_chunk = np.arange(WINDOW + group)[None, :] // CHUNK
    visible = (k_chunk >= q_chunk) & (k_chunk <= q_chunk + WINDOW // CHUNK)
    at_start = visible & (k_chunk >= WINDOW // CHUNK)
    return jnp.asarray(np.where(np.stack([visible, at_start]), 0.0, -np.inf), dtype=F32)


def _to_head_pairs(w, axis):
    lead, trail = w.shape[:axis], w.shape[axis + 1:]
    w = w.reshape(lead + (ATTN_KV_HEADS, ATTN_GROUP, ATTN_HEAD_DIM) + trail)
    return jnp.swapaxes(w, axis, axis + 1).reshape(lead + (Q_A,) + trail)


def _layer(x, p, pos, past, w):
    (g_mix_pre, w_in, attn_sinks, w_branch_attn, w_branch_ret, w_out, g_mix_post, g_ffn_pre,
     w_ffn_gate, w_ffn_up, w_ffn_down, g_ffn_post, w_ple_proj, w_ple_gate) = w
    batch, seq_len, d_model = x.shape
    n = batch * seq_len
    x2 = x.reshape(n, d_model)
    p2 = p.reshape(n, p.shape[-1])

    w_in_b = w_in.astype(BF16)
    wq_b = _to_head_pairs(w_in[:, :Q_A], 1).astype(BF16)
    cos_tab, sin_tab = _rotary_tables(pos)
    g_pre = g_mix_pre.reshape(1, d_model)
    sinks = attn_sinks.astype(F32)
    if past is None and seq_len >= TOKEN_TILE:
        oa, orr, gates, kv32, state = _front_fused(
            x2, g_pre, wq_b, w_in_b, cos_tab, sin_tab, sinks, batch, seq_len)
    else:
        ret_block = min(RET_BLOCK, MIXER_TILE, seq_len)
        z, gates, kv32 = _in_projection(x2, g_pre, wq_b, w_in_b, cos_tab, sin_tab, seq_len, ret_block)
        past_args = None
        if past is not None:
            cache_k, cache_v, state0 = past
            past_args = (cache_k.reshape(batch, WINDOW, KV_A).astype(BF16),
                         cache_v.reshape(batch, WINDOW, KV_A).astype(BF16), state0.astype(F32))
        oa, orr, state = _mixers(z, sinks, batch, seq_len, past_args)

    weights = (_to_head_pairs(w_branch_attn, 0).astype(BF16), w_branch_ret.astype(BF16), w_out.astype(BF16),
               g_mix_post.reshape(1, d_model), g_ffn_pre.reshape(1, d_model),
               w_ffn_gate.astype(BF16), w_ffn_up.astype(BF16), w_ffn_down.astype(BF16),
               g_ffn_post.reshape(1, d_model), w_ple_proj.astype(BF16), w_ple_gate.astype(BF16))
    y = _output_stage(oa, orr, gates, x2, p2, weights, batch, seq_len)

    tail = kv32.shape[0] // batch
    kv32 = kv32.reshape(batch, tail, 2, ATTN_KV_HEADS, ATTN_HEAD_DIM)
    k_new, v_new = kv32[:, :, 0], kv32[:, :, 1]
    if past is not None:
        n_win = past[0].shape[1]
        k_new = jnp.concatenate([past[0], k_new], axis=1)[:, -n_win:]
        v_new = jnp.concatenate([past[1], v_new], axis=1)[:, -n_win:]
    return y, k_new, v_new, state


def kernel(x_prompt, x_sample, p_prompt, p_sample, cache_attn_k, cache_attn_v, state_ret, g_mix_pre, w_in, attn_sinks, w_branch_attn, w_branch_ret, w_out, g_mix_post, g_ffn_pre, w_ffn_gate, w_ffn_up, w_ffn_down, g_ffn_post, w_ple_proj, w_ple_gate):
    depth = w_in.shape[0]
    assert cache_attn_k.shape[2] == WINDOW, "the rolling window must be full"
    pos_prompt = np.arange(x_prompt.shape[1])
    pos_sample = PAST_LEN + np.arange(x_sample.shape[1])
    y_p, y_s = x_prompt, x_sample
    outs = [[] for _ in range(6)]
    for i in range(depth):
        w_i = (g_mix_pre[i], w_in[i], attn_sinks[i], w_branch_attn[i], w_branch_ret[i], w_out[i],
               g_mix_post[i], g_ffn_pre[i], w_ffn_gate[i], w_ffn_up[i], w_ffn_down[i], g_ffn_post[i],
               w_ple_proj[i], w_ple_gate[i])
        y_p, kp, vp, rp = _layer(y_p, p_prompt[i], pos_prompt, None, w_i)
        y_s, ks, vs, rs = _layer(y_s, p_sample[i], pos_sample,
                                 (cache_attn_k[i], cache_attn_v[i], state_ret[i]), w_i)
        for lst, val in zip(outs, (kp, vp, rp.astype(x_prompt.dtype), ks, vs, rs.astype(x_sample.dtype))):
            lst.append(val)
    return (y_p, y_s) + tuple(jnp.stack(lst) for lst in outs)
```

```python
import functools

import jax
import jax.numpy as jnp
import numpy as np
from jax import lax
from jax.experimental import pallas as pl
from jax.experimental.pallas import tpu as pltpu

CHUNK = 64
WINDOW = 128
ATTN_HEADS = 8
ATTN_KV_HEADS = 2
ATTN_GROUP = ATTN_HEADS // ATTN_KV_HEADS
ATTN_HEAD_DIM = 64
RET_HEADS = 4
RET_KEY_DIM = 128
RET_VALUE_DIM = 256
RET_ROPE_BASE = 10000.0
NORM_EPS = 1e-6
GN_EPS = 1e-5
PAST_LEN = 2048

Q_A = ATTN_HEADS * ATTN_HEAD_DIM
HEAD_PAIR = 2 * ATTN_HEAD_DIM
KV_A = ATTN_KV_HEADS * ATTN_HEAD_DIM
QK_R = RET_HEADS * RET_KEY_DIM
V_R = RET_HEADS * RET_VALUE_DIM

QA_OFF = 0
KA_OFF = QA_OFF + Q_A
VA_OFF = KA_OFF + KV_A
QR_OFF = VA_OFF + KV_A
KR_OFF = QR_OFF + QK_R
VR_OFF = KR_OFF + QK_R
GR_OFF = VR_OFF + V_R
Z_WIDTH = GR_OFF + V_R

VMEM_LIMIT_BYTES = 56 * 1024 * 1024
TOKEN_TILE = 512
MIXER_TILE = 256
SHORT_SEQS_PER_STEP = 4
RET_BLOCK = 256
ATT_GROUP = 128
FFN_CHUNK = 512
POST_TILE = 1024

BF16 = jnp.bfloat16
F32 = jnp.float32
LOG2E = float(np.log2(np.e))


def _dot(a, b):
    return jnp.dot(a, b, preferred_element_type=F32)


def _dot_nt(a, b):
    return lax.dot_general(a, b, (((1,), (1,)), ((), ())), preferred_element_type=F32)


def _dot_tn(a, b):
    return lax.dot_general(a, b, (((0,), (0,)), ((), ())), preferred_element_type=F32)


def _sigmoid(x):
    return 0.5 * jnp.tanh(0.5 * x) + 0.5


def _rmsnorm(x, g):
    return x * lax.rsqrt(jnp.mean(x * x, axis=-1, keepdims=True) + NORM_EPS) * g


def _resident(shape):
    return pl.BlockSpec(shape, lambda *_: (0,) * len(shape), pipeline_mode=pl.Buffered(1))


def _project_qk(x_ref, g_ref, wq_ref, w_ref, cos_ref, sin_ref, qdec_ref, kdec_ref, z_ref):
    h = _rmsnorm(x_ref[...], g_ref[...]).astype(BF16)

    z_ref[:, QA_OFF:QA_OFF + Q_A] = (_dot(h, wq_ref[...]) * (ATTN_HEAD_DIM ** -0.5 * LOG2E)).astype(BF16)
    kv = _dot(h, w_ref[:, KA_OFF:KA_OFF + 2 * KV_A])
    z_ref[:, KA_OFF:KA_OFF + 2 * KV_A] = kv.astype(BF16)

    cos = cos_ref[...]
    sin = sin_ref[...]
    for off, dec_ref in ((QR_OFF, qdec_ref), (KR_OFF, kdec_ref)):
        z = _dot(h, w_ref[:, off:off + QK_R])
        for hh in range(RET_HEADS):
            zh = z[:, hh * RET_KEY_DIM:(hh + 1) * RET_KEY_DIM]
            rot = (zh * cos + pltpu.roll(zh, RET_KEY_DIM // 2, 1) * sin) * dec_ref[hh]
            z_ref[:, off + hh * RET_KEY_DIM:off + (hh + 1) * RET_KEY_DIM] = rot.astype(BF16)
    return h, kv


def _project_rest(h, w_ref, z_ref, gates_ref):
    z_ref[:, VR_OFF:VR_OFF + V_R] = _dot(h, w_ref[:, VR_OFF:VR_OFF + V_R]).astype(BF16)
    g = _dot(h, w_ref[:, GR_OFF:GR_OFF + V_R])
    z_ref[:, GR_OFF:GR_OFF + V_R] = (g * _sigmoid(g)).astype(BF16)
    d_model = gates_ref.shape[1] // 2
    for j in range(2):
        gates_ref[:, j * d_model:(j + 1) * d_model] = _dot(
            h, w_ref[:, Z_WIDTH + j * d_model:Z_WIDTH + (j + 1) * d_model]).astype(BF16)


def _attend(z_ref, sink_ref, bias_ref, oa_ref, kbuf, vbuf, *, tile, att_group, use_bias, seq_start):
    kbuf[WINDOW:, :] = z_ref[:, KA_OFF:KA_OFF + KV_A]
    vbuf[WINDOW:, 0:KV_A] = z_ref[:, VA_OFF:VA_OFF + KV_A]

    low_half = lax.broadcasted_iota(jnp.int32, (att_group, HEAD_PAIR), 1) < ATTN_HEAD_DIM
    win = WINDOW + att_group
    for gi in range(tile // att_group):
        rows = slice(gi * att_group, (gi + 1) * att_group)
        q_tiles = [z_ref[rows, QA_OFF + j * HEAD_PAIR:QA_OFF + (j + 1) * HEAD_PAIR] for j in range(ATTN_GROUP)]
        zero = jnp.zeros_like(q_tiles[0])
        lhs = jnp.concatenate([jnp.where(low_half, q, zero) for q in q_tiles]
                              + [jnp.where(low_half, zero, q) for q in q_tiles], axis=0)
        k_win = kbuf[gi * att_group:gi * att_group + win, :]
        v_win = vbuf[gi * att_group:gi * att_group + win, :]
        s_all = _dot_nt(lhs, k_win)
        if use_bias:
            bias = bias_ref[0] if (seq_start is None or gi > 0) else bias_ref[jnp.where(seq_start, 1, 0)]
        probs, sink_terms = [], []
        for r in range(ATTN_HEADS):
            s = s_all[r * att_group:(r + 1) * att_group]
            if use_bias:
                s = s + bias
            sink = sink_ref[r] * LOG2E
            m = jnp.maximum(jnp.max(s, axis=1, keepdims=True), sink)
            sink_terms.append(jnp.exp2(sink - m))
            probs.append(jnp.exp2(s - m).astype(BF16))
        o = _dot(jnp.concatenate(probs, axis=0), v_win)

        def head_out(r):
            blk = o[r * att_group:(r + 1) * att_group]
            return blk[:, 0:KV_A] * (1.0 / (blk[:, KV_A:] + sink_terms[r]))

        for j in range(ATTN_GROUP):
            oa_ref[rows, j * HEAD_PAIR:(j + 1) * HEAD_PAIR] = jnp.where(
                low_half, head_out(j), head_out(ATTN_GROUP + j)).astype(BF16)

    kbuf[0:WINDOW, :] = kbuf[tile:tile + WINDOW, :]
    vbuf[0:WINDOW, 0:KV_A] = vbuf[tile:tile + WINDOW, 0:KV_A]


def _retain(z_ref, tril_ref, or_ref, state, *, tile, ret_block, state_decay):
    for bi in range(tile // ret_block):
        rows = slice(bi * ret_block, (bi + 1) * ret_block)
        for hh in range(RET_HEADS):
            vcols = slice(hh * RET_VALUE_DIM, (hh + 1) * RET_VALUE_DIM)
            q = z_ref[rows, QR_OFF + hh * RET_KEY_DIM:QR_OFF + (hh + 1) * RET_KEY_DIM]
            k = z_ref[rows, KR_OFF + hh * RET_KEY_DIM:KR_OFF + (hh + 1) * RET_KEY_DIM]
            v = z_ref[rows, VR_OFF + hh * RET_VALUE_DIM:VR_OFF + (hh + 1) * RET_VALUE_DIM]
            st = state[hh]
            o_r = _dot((_dot_nt(q, k) * tril_ref[...]).astype(BF16), v) + _dot(q, st.astype(BF16))
            state[hh] = state_decay[hh] * (st + _dot_tn(k, v))
            mu = jnp.mean(o_r, axis=1, keepdims=True)
            cen = o_r - mu
            var = jnp.mean(cen * cen, axis=1, keepdims=True)
            gate = z_ref[rows, GR_OFF + hh * RET_VALUE_DIM:GR_OFF + (hh + 1) * RET_VALUE_DIM].astype(F32)
            or_ref[rows, vcols] = (cen * lax.rsqrt(var + GN_EPS) * gate).astype(BF16)


def _init_window(kbuf, vbuf, k_rows, v_rows):
    kbuf[0:WINDOW, :] = k_rows
    vbuf[0:WINDOW, 0:KV_A] = v_rows
    vbuf[:, KV_A:] = jnp.ones((vbuf.shape[0], KV_A), BF16)


def _mix_config(tile):
    ret_block = min(RET_BLOCK, tile)
    att_group = min(ATT_GROUP, tile)
    assert tile % ret_block == 0 and tile % att_group == 0 and att_group % CHUNK == 0
    tril = jnp.asarray(np.tril(np.ones((ret_block, ret_block))), dtype=F32)
    tables = (_attention_bias(att_group), tril)
    att_cfg = dict(tile=tile, att_group=att_group)
    ret_cfg = dict(tile=tile, ret_block=ret_block, state_decay=_state_decay(ret_block))
    return att_cfg, ret_cfg, tables


STATE_SHAPE = (RET_HEADS, RET_KEY_DIM, RET_VALUE_DIM)


def _front_kernel(x_ref, g_ref, wq_ref, w_ref, cos_ref, sin_ref, qdec_ref, kdec_ref, sink_ref, bias_ref, tril_ref,
                  oa_ref, or_ref, gates_ref, kv32_ref, st_out_ref,
                  z_even, z_odd, kbuf, vbuf, state, *, tiles_per_seq, tail_rows, att_cfg, ret_cfg):
    i = pl.program_id(0)
    tile = att_cfg["tile"]
    no_rows = jnp.zeros((WINDOW, KV_A), BF16)

    mix_tile_in_seq = jnp.maximum(i - 1, 0) % tiles_per_seq
    seq_start = mix_tile_in_seq == 0

    @pl.when(i == 0)
    def _():
        z_odd[...] = jnp.zeros(z_odd.shape, BF16)
        vbuf[:, KV_A:] = jnp.ones((vbuf.shape[0], KV_A), BF16)

    @pl.when(seq_start)
    def _():
        kbuf[0:WINDOW, :] = no_rows
        vbuf[0:WINDOW, 0:KV_A] = no_rows
        state[...] = jnp.zeros(state.shape, F32)

    def step(z_write, z_read):
        _retain(z_read, tril_ref, or_ref, state, **ret_cfg)
        h, kv = _project_qk(x_ref, g_ref, wq_ref, w_ref, cos_ref, sin_ref, qdec_ref, kdec_ref, z_write)
        kv32_ref[...] = kv[tile - tail_rows:, :]
        _attend(z_read, sink_ref, bias_ref, oa_ref, kbuf, vbuf, use_bias=True, seq_start=seq_start, **att_cfg)
        _project_rest(h, w_ref, z_write, gates_ref)

    pl.when(i % 2 == 0)(lambda: step(z_even, z_odd))
    pl.when(i % 2 == 1)(lambda: step(z_odd, z_even))

    @pl.when(mix_tile_in_seq == tiles_per_seq - 1)
    def _():
        st_out_ref[0] = state[...]


def _front_fused(x2, g_pre, wq_bf16, w_in_bf16, cos_tab, sin_tab, sinks, batch, seq_len):
    n, d_model = x2.shape
    d_in = w_in_bf16.shape[1]
    tile = TOKEN_TILE
    assert seq_len % tile == 0 and tile >= WINDOW
    tiles_per_seq = seq_len // tile
    n_tiles = n // tile
    tail_rows = WINDOW
    att_cfg, ret_cfg, tables = _mix_config(tile)
    dec_tabs = _projection_decay(tile, ret_cfg["ret_block"])

    def proj_tile(i):
        return jnp.minimum(i, n_tiles - 1)

    def mix_tile(i):
        return jnp.maximum(i - 1, 0)

    def mixed_rows(width):
        return pl.BlockSpec((tile, width), lambda i: (mix_tile(i), 0))

    in_specs = [pl.BlockSpec((tile, d_model), lambda i: (proj_tile(i), 0)),
                _resident((1, d_model)), _resident((d_model, Q_A)), _resident((d_model, d_in)),
                pl.BlockSpec((tile, RET_KEY_DIM), lambda i: (proj_tile(i) % tiles_per_seq, 0)),
                pl.BlockSpec((tile, RET_KEY_DIM), lambda i: (proj_tile(i) % tiles_per_seq, 0)),
                _resident(dec_tabs[0].shape), _resident(dec_tabs[1].shape),
                pl.BlockSpec(memory_space=pltpu.SMEM)]
    in_specs += [_resident(tab.shape) for tab in tables]
    out_specs = [mixed_rows(Q_A), mixed_rows(V_R),
                 pl.BlockSpec((tile, 2 * d_model), lambda i: (proj_tile(i), 0)),
                 pl.BlockSpec((tail_rows, 2 * KV_A), lambda i: (proj_tile(i) // tiles_per_seq, 0)),
                 pl.BlockSpec((1,) + STATE_SHAPE, lambda i: (mix_tile(i) // tiles_per_seq, 0, 0, 0))]
    out_shape = [jax.ShapeDtypeStruct((n, Q_A), BF16), jax.ShapeDtypeStruct((n, V_R), BF16),
                 jax.ShapeDtypeStruct((n, 2 * d_model), BF16),
                 jax.ShapeDtypeStruct((batch * tail_rows, 2 * KV_A), F32),
                 jax.ShapeDtypeStruct((batch,) + STATE_SHAPE, F32)]
    return pl.pallas_call(
        functools.partial(_front_kernel, tiles_per_seq=tiles_per_seq, tail_rows=tail_rows,
                          att_cfg=att_cfg, ret_cfg=ret_cfg),
        grid=(n_tiles + 1,),
        in_specs=in_specs,
        out_specs=out_specs,
        out_shape=out_shape,
        scratch_shapes=[pltpu.VMEM((tile, Z_WIDTH), BF16), pltpu.VMEM((tile, Z_WIDTH), BF16),
                        pltpu.VMEM((WINDOW + tile, KV_A), BF16), pltpu.VMEM((WINDOW + tile, 2 * KV_A), BF16),
                        pltpu.VMEM(STATE_SHAPE, F32)],
        compiler_params=pltpu.CompilerParams(
            dimension_semantics=("arbitrary",), vmem_limit_bytes=VMEM_LIMIT_BYTES),
        name="front",
    )(x2, g_pre, wq_bf16, w_in_bf16, cos_tab, sin_tab, *dec_tabs, sinks, *tables)


def _inproj_kernel(x_ref, g_ref, wq_ref, w_ref, cos_ref, sin_ref, qdec_ref, kdec_ref, z_ref, gates_ref, kv32_ref):
    h, kv = _project_qk(x_ref, g_ref, wq_ref, w_ref, cos_ref, sin_ref, qdec_ref, kdec_ref, z_ref)
    kv32_ref[...] = kv
    _project_rest(h, w_ref, z_ref, gates_ref)


def _in_projection(x2, g_pre, wq_bf16, w_in_bf16, cos_tab, sin_tab, seq_len, ret_block):
    n, d_model = x2.shape
    d_in = w_in_bf16.shape[1]
    tile = min(TOKEN_TILE, n)
    assert n % tile == 0 and tile % seq_len == 0 and seq_len <= WINDOW and seq_len % ret_block == 0
    cos_tab = np.tile(cos_tab, (tile // seq_len, 1))
    sin_tab = np.tile(sin_tab, (tile // seq_len, 1))
    dec_tabs = _projection_decay(tile, ret_block)

    def row_spec(width):
        return pl.BlockSpec((tile, width), lambda i: (i, 0))

    tab_spec = pl.BlockSpec((tile, RET_KEY_DIM), lambda i: (0, 0))
    out_widths = (Z_WIDTH, 2 * d_model)
    out_shape = [jax.ShapeDtypeStruct((n, w), BF16) for w in out_widths]
    out_shape.append(jax.ShapeDtypeStruct((n, 2 * KV_A), F32))
    return pl.pallas_call(
        _inproj_kernel,
        grid=(n // tile,),
        in_specs=[row_spec(d_model), _resident((1, d_model)), _resident((d_model, Q_A)),
                  _resident((d_model, d_in)), tab_spec, tab_spec,
                  _resident(dec_tabs[0].shape), _resident(dec_tabs[1].shape)],
        out_specs=[row_spec(w) for w in out_widths] + [row_spec(2 * KV_A)],
        out_shape=out_shape,
        compiler_params=pltpu.CompilerParams(
            dimension_semantics=("arbitrary",), vmem_limit_bytes=VMEM_LIMIT_BYTES),
        name="in_projection",
    )(x2, g_pre, wq_bf16, w_in_bf16, cos_tab, sin_tab, *dec_tabs)


def _mixer_kernel(*refs, seqs, has_past, use_bias, att_cfg, ret_cfg):
    z_ref, sink_ref, bias_ref, tril_ref = refs[:4]
    rest = refs[4:]
    if has_past:
        ck_ref, cv_ref, st0_ref = rest[:3]
        rest = rest[3:]
    oa_ref, or_ref, st_out_ref, kbuf, vbuf, state = rest
    t = pl.program_id(1)
    tile = att_cfg["tile"]

    @pl.when(t == 0)
    def _():
        for s in range(seqs):
            if has_past:
                _init_window(kbuf.at[s], vbuf.at[s], ck_ref[s], cv_ref[s])
                state[s] = st0_ref[s]
            else:
                no_rows = jnp.zeros((WINDOW, KV_A), BF16)
                _init_window(kbuf.at[s], vbuf.at[s], no_rows, no_rows)
                state[s] = jnp.zeros(STATE_SHAPE, F32)

    for s in range(seqs):
        rows = pl.ds(s * tile, tile)
        _attend(z_ref.at[rows, :], sink_ref, bias_ref, oa_ref.at[rows, :], kbuf.at[s], vbuf.at[s],
                use_bias=use_bias, seq_start=None if has_past else t == 0, **att_cfg)
        _retain(z_ref.at[rows, :], tril_ref, or_ref.at[rows, :], state.at[s], **ret_cfg)

    @pl.when(t == pl.num_programs(1) - 1)
    def _():
        st_out_ref[...] = state[...]


def _mixers(z, sinks, batch, seq_len, past=None):
    tile = min(MIXER_TILE, seq_len)
    assert seq_len % tile == 0
    nt = seq_len // tile
    n = batch * seq_len
    seqs = SHORT_SEQS_PER_STEP if nt == 1 and batch % SHORT_SEQS_PER_STEP == 0 else 1
    att_cfg, ret_cfg, tables = _mix_config(tile)
    use_bias = not (past is not None and att_cfg["att_group"] == CHUNK)

    def row_spec(width):
        return pl.BlockSpec((seqs * tile, width), lambda b, t: (b * nt + t, 0))

    def per_seq_spec(shape):
        return pl.BlockSpec((seqs,) + shape, lambda b, t: (b,) + (0,) * len(shape))

    in_specs = [row_spec(Z_WIDTH), pl.BlockSpec(memory_space=pltpu.SMEM)]
    in_specs += [_resident(tab.shape) for tab in tables]
    args = [z, sinks] + list(tables)
    if past is not None:
        in_specs += [per_seq_spec((WINDOW, KV_A)), per_seq_spec((WINDOW, KV_A)), per_seq_spec(STATE_SHAPE)]
        args += list(past)
    return pl.pallas_call(
        functools.partial(_mixer_kernel, seqs=seqs, has_past=past is not None, use_bias=use_bias,
                          att_cfg=att_cfg, ret_cfg=ret_cfg),
        grid=(batch // seqs, nt),
        in_specs=in_specs,
        out_specs=[row_spec(Q_A), row_spec(V_R), per_seq_spec(STATE_SHAPE)],
        out_shape=[jax.ShapeDtypeStruct((n, Q_A), BF16), jax.ShapeDtypeStruct((n, V_R), BF16),
                   jax.ShapeDtypeStruct((batch,) + STATE_SHAPE, F32)],
        scratch_shapes=[pltpu.VMEM((seqs, WINDOW + tile, KV_A), BF16),
                        pltpu.VMEM((seqs, WINDOW + tile, 2 * KV_A), BF16),
                        pltpu.VMEM((seqs,) + STATE_SHAPE, F32)],
        compiler_params=pltpu.CompilerParams(
            dimension_semantics=("arbitrary", "arbitrary"), vmem_limit_bytes=VMEM_LIMIT_BYTES),
        name="mixers",
    )(*args)


def _merge_kernel(oa_ref, or_ref, gates_ref, x_ref, wba_ref, wbr_ref, wout_ref, gpost_ref, gfpre_ref,
                  y_ref, h_ref):
    d_model = x_ref.shape[1]
    gate_a = gates_ref[:, :d_model].astype(F32)
    gate_r = gates_ref[:, d_model:].astype(F32)
    merged = (_sigmoid(gate_a) * _dot(oa_ref[...], wba_ref[...])
              + _sigmoid(gate_r) * _dot(or_ref[...], wbr_ref[...]))
    y = x_ref[...] + _rmsnorm(_dot(merged.astype(BF16), wout_ref[...]), gpost_ref[...])
    y_ref[...] = y
    h_ref[...] = (y * gfpre_ref[...]).astype(BF16)


def _ffn_kernel(y1_ref, h_ref, p_ref, wg_ref, wu_ref, wd_ref, gfpost_ref, wpp_ref, wpg_ref, y_ref, *, ffn_slabs):
    y = y1_ref[...]
    h = h_ref[...]
    row_scale = lax.rsqrt(jnp.mean(y * y, axis=-1, keepdims=True) + NORM_EPS)
    f = None
    for lo, hi in ffn_slabs:
        gate = _dot(h, wg_ref[:, lo:hi]) * row_scale
        act = (gate * _sigmoid(gate) * (_dot(h, wu_ref[:, lo:hi]) * row_scale)).astype(BF16)
        part = _dot(act, wd_ref[lo:hi, :])
        f = part if f is None else f + part
    y = y + _rmsnorm(f, gfpost_ref[...])

    emb = _dot(p_ref[...].astype(BF16), wpp_ref[...])
    y = y + emb * _sigmoid(_dot(y.astype(BF16), wpg_ref[...]))
    y_ref[...] = y.reshape(y_ref.shape)


def _output_stage(oa, orr, gates, x2, p2, weights, batch, seq_len):
    (wba, wbr, wout, gpost, gfpre, wg, wu, wd, gfpost, wpp, wpg) = weights
    n, d_model = x2.shape
    tile = min(POST_TILE, n)
    assert n % tile == 0
    ffn_hidden = wd.shape[0]
    ffn_slabs = tuple((lo, min(lo + FFN_CHUNK, ffn_hidden)) for lo in range(0, ffn_hidden, FFN_CHUNK))

    def row_spec(width):
        return pl.BlockSpec((tile, width), lambda i: (i, 0))

    if seq_len >= tile:
        assert seq_len % tile == 0
        tiles_per_seq = seq_len // tile
        out_spec = pl.BlockSpec((1, tile, d_model), lambda i: (i // tiles_per_seq, i % tiles_per_seq, 0))
    else:
        assert tile % seq_len == 0
        out_spec = pl.BlockSpec((tile // seq_len, seq_len, d_model), lambda i: (i, 0, 0))

    params = pltpu.CompilerParams(dimension_semantics=("arbitrary",), vmem_limit_bytes=VMEM_LIMIT_BYTES)
    merge_acts = (oa, orr, gates, x2)
    merge_w = (wba, wbr, wout, gpost, gfpre)
    y1, h = pl.pallas_call(
        _merge_kernel,
        grid=(n // tile,),
        in_specs=[row_spec(a.shape[1]) for a in merge_acts] + [_resident(w.shape) for w in merge_w],
        out_specs=[row_spec(d_model), row_spec(d_model)],
        out_shape=[jax.ShapeDtypeStruct((n, d_model), F32), jax.ShapeDtypeStruct((n, d_model), BF16)],
        compiler_params=params,
        name="merge_stage",
    )(*merge_acts, *merge_w)
    ffn_acts = (y1, h, p2)
    ffn_w = (wg, wu, wd, gfpost, wpp, wpg)
    return pl.pallas_call(
        functools.partial(_ffn_kernel, ffn_slabs=ffn_slabs),
        grid=(n // tile,),
        in_specs=[row_spec(a.shape[1]) for a in ffn_acts] + [_resident(w.shape) for w in ffn_w],
        out_specs=out_spec,
        out_shape=jax.ShapeDtypeStruct((batch, seq_len, d_model), F32),
        compiler_params=params,
        name="ffn_stage",
    )(*ffn_acts, *ffn_w)


def _rotary_tables(pos):
    half = RET_KEY_DIM // 2
    inv = 1.0 / (RET_ROPE_BASE ** np.linspace(0.0, 1.0, half))
    ang = np.asarray(pos, np.float64)[:, None] * inv[None, :]
    cos, sin = np.cos(ang), np.sin(ang)
    return (np.concatenate([cos, cos], axis=1).astype(np.float32),
            np.concatenate([-sin, sin], axis=1).astype(np.float32))


def _log_decay():
    return np.log1p(-np.exp2(-5.0 - np.arange(RET_HEADS, dtype=np.float64)))


def _projection_decay(tile, block):
    pos = (np.arange(tile) % block) + 1.0
    q_dec = np.exp(_log_decay()[:, None] * pos[None, :])
    k_dec = np.exp(-_log_decay()[:, None] * pos[None, :]) * (RET_KEY_DIM ** -0.5)
    shape = (RET_HEADS, tile, RET_KEY_DIM)
    return tuple(np.ascontiguousarray(np.broadcast_to(t[:, :, None], shape), dtype=np.float32)
                 for t in (q_dec, k_dec))


def _state_decay(block):
    return tuple(float(v) for v in np.exp(_log_decay() * block))


def _attention_bias(group):
    q_chunk = np.arange(group)[:, None] // CHUNK
    k_chunk = np.arange(WINDOW + group)[None, :] // CHUNK
    visible = (k_chunk >= q_chunk) & (k_chunk <= q_chunk + WINDOW // CHUNK)
    at_start = visible & (k_chunk >= WINDOW // CHUNK)
    return jnp.asarray(np.where(np.stack([visible, at_start]), 0.0, -np.inf), dtype=F32)


def _to_head_pairs(w, axis):
    lead, trail = w.shape[:axis], w.shape[axis + 1:]
    w = w.reshape(lead + (ATTN_KV_HEADS, ATTN_GROUP, ATTN_HEAD_DIM) + trail)
    return jnp.swapaxes(w, axis, axis + 1).reshape(lead + (Q_A,) + trail)


def _layer(x, p, pos, past, w):
    (g_mix_pre, w_in, attn_sinks, w_branch_attn, w_branch_ret, w_out, g_mix_post, g_ffn_pre,
     w_ffn_gate, w_ffn_up, w_ffn_down, g_ffn_post, w_ple_proj, w_ple_gate) = w
    batch, seq_len, d_model = x.shape
    n = batch * seq_len
    x2 = x.reshape(n, d_model)
    p2 = p.reshape(n, p.shape[-1])

    w_in_b = w_in.astype(BF16)
    wq_b = _to_head_pairs(w_in[:, :Q_A], 1).astype(BF16)
    cos_tab, sin_tab = _rotary_tables(pos)
    g_pre = g_mix_pre.reshape(1, d_model)
    sinks = attn_sinks.astype(F32)
    if past is None and seq_len >= TOKEN_TILE:
        oa, orr, gates, kv32, state = _front_fused(
            x2, g_pre, wq_b, w_in_b, cos_tab, sin_tab, sinks, batch, seq_len)
    else:
        ret_block = min(RET_BLOCK, MIXER_TILE, seq_len)
        z, gates, kv32 = _in_projection(x2, g_pre, wq_b, w_in_b, cos_tab, sin_tab, seq_len, ret_block)
        past_args = None
        if past is not None:
            cache_k, cache_v, state0 = past
            past_args = (cache_k.reshape(batch, WINDOW, KV_A).astype(BF16),
                         cache_v.reshape(batch, WINDOW, KV_A).astype(BF16), state0.astype(F32))
        oa, orr, state = _mixers(z, sinks, batch, seq_len, past_args)

    weights = (_to_head_pairs(w_branch_attn, 0).astype(BF16), w_branch_ret.astype(BF16), w_out.astype(BF16),
               g_mix_post.reshape(1, d_model), g_ffn_pre.reshape(1, d_model),
               w_ffn_gate.astype(BF16), w_ffn_up.astype(BF16), w_ffn_down.astype(BF16),
               g_ffn_post.reshape(1, d_model), w_ple_proj.astype(BF16), w_ple_gate.astype(BF16))
    y = _output_stage(oa, orr, gates, x2, p2, weights, batch, seq_len)

    tail = kv32.shape[0] // batch
    kv32 = kv32.reshape(batch, tail, 2, ATTN_KV_HEADS, ATTN_HEAD_DIM)
    k_new, v_new = kv32[:, :, 0], kv32[:, :, 1]
    if past is not None:
        n_win = past[0].shape[1]
        k_new = jnp.concatenate([past[0], k_new], axis=1)[:, -n_win:]
        v_new = jnp.concatenate([past[1], v_new], axis=1)[:, -n_win:]
    return y, k_new, v_new, state


def kernel(x_prompt, x_sample, p_prompt, p_sample, cache_attn_k, cache_attn_v, state_ret, g_mix_pre, w_in, attn_sinks, w_branch_attn, w_branch_ret, w_out, g_mix_post, g_ffn_pre, w_ffn_gate, w_ffn_up, w_ffn_down, g_ffn_post, w_ple_proj, w_ple_gate):
    depth = w_in.shape[0]
    assert cache_attn_k.shape[2] == WINDOW, "the rolling window must be full"
    pos_prompt = np.arange(x_prompt.shape[1])
    pos_sample = PAST_LEN + np.arange(x_sample.shape[1])
    y_p, y_s = x_prompt, x_sample
    outs = [[] for _ in range(6)]
    for i in range(depth):
        w_i = (g_mix_pre[i], w_in[i], attn_sinks[i], w_branch_attn[i], w_branch_ret[i], w_out[i],
               g_mix_post[i], g_ffn_pre[i], w_ffn_gate[i], w_ffn_up[i], w_ffn_down[i], g_ffn_post[i],
               w_ple_proj[i], w_ple_gate[i])
        y_p, kp, vp, rp = _layer(y_p, p_prompt[i], pos_prompt, None, w_i)
        y_s, ks, vs, rs = _layer(y_s, p_sample[i], pos_sample,
                                 (cache_attn_k[i], cache_attn_v[i], state_ret[i]), w_i)
        for lst, val in zip(outs, (kp, vp, rp.astype(x_prompt.dtype), ks, vs, rs.astype(x_sample.dtype))):
            lst.append(val)
    return (y_p, y_s) + tuple(jnp.stack(lst) for lst in outs)
```
